```python
import math
import jax
import jax.numpy as jnp
from jax import lax
import numpy as np

D_MODEL = 1024
BATCH = 32
SEQ = 256
DEPTH = 2
DEC_BATCH = 8
DEC_SEQ = 2048
PAST_LEN = 512

GRID_W = 64
ROPE_BASE = 10000.0
EPS = 1e-6
Q_BLOCK = 128
N_ATTN = (DEPTH + 1) // 2
N_SSD = DEPTH // 2

A_HEADS = 4
A_DQK = 64
A_DV = 2 * A_DQK
A_QK_W = A_HEADS * 2 * A_DQK
A_V_W = A_HEADS * A_DV
B_HEADS = 4
B_NOPE = 128
B_ROPE = 64
B_DQK = B_NOPE + B_ROPE
B_DV = 128
B_Q_LORA = 384
B_KV_LORA = 256
ATTN_SPLITS = (A_QK_W, 2 * A_QK_W, 2 * A_QK_W + A_V_W, 2 * A_QK_W + A_V_W + B_Q_LORA,
               2 * A_QK_W + A_V_W + B_Q_LORA + B_KV_LORA)
ATTN_IN = 2 * A_QK_W + A_V_W + B_Q_LORA + B_KV_LORA + B_ROPE
ATTN_OUT = A_HEADS * A_DV + B_HEADS * B_DV
SSD_DI = 2 * D_MODEL
SSD_P = 64
SSD_H = SSD_DI // SSD_P
SSD_N = 128
SSD_G = 4
SSD_CONV = 3
SSD_CHUNK = 128
SSD_CONV_DIM = SSD_DI + 2 * SSD_G * SSD_N
SSD_IN = SSD_DI + SSD_CONV_DIM + 2 * SSD_H
SSD_SPLITS = (SSD_DI, SSD_DI + SSD_CONV_DIM)
XBC_SPLITS = (SSD_DI, SSD_DI + SSD_G * SSD_N)
N_EXPERTS = 16
N_EXPERT_GROUPS = 4
EXPERTS_PER_GROUP = N_EXPERTS // N_EXPERT_GROUPS
TOP_K = 2
D_EXPERT = 256

kernel_name = 'hybrid_diffattn_mla_ssd_moe_dit_step'


def rmsnorm(x, g=None):
    xf = x.astype(jnp.float32)
    y = xf * lax.rsqrt(jnp.mean(xf * xf, axis=-1, keepdims=True) + EPS)
    if g is not None:
        y = y * g.astype(jnp.float32)
    return y.astype(x.dtype)


def modulate(x, g, shift, scale):
    return rmsnorm(x, g) * (1.0 + scale) + shift


def axial_rope(x):
    n_tok = x.shape[1]
    rows = n_tok // GRID_W
    t = jnp.arange(rows * GRID_W)
    row = (t // GRID_W).astype(jnp.float32)
    col = (t % GRID_W).astype(jnp.float32)
    half = x.shape[-1] // 2
    inv = ROPE_BASE ** (-2.0 * jnp.arange(half // 2, dtype=jnp.float32) / half)
    ang = jnp.concatenate([row[:, None] * inv, col[:, None] * inv], axis=-1)
    ang = ang.reshape((n_tok,) + (1,) * (x.ndim - 3) + (half,))
    cos, sin = jnp.cos(ang), jnp.sin(ang)
    xf = x.astype(jnp.float32)
    x1, x2 = xf[..., :half], xf[..., half:]
    return jnp.concatenate([x1 * cos - x2 * sin, x1 * sin + x2 * cos], axis=-1).astype(x.dtype)


def map_query_blocks(fn, q):
    bsz, n = q.shape[:2]
    nb = n // Q_BLOCK
    qb = jnp.moveaxis(q.reshape((bsz, nb, Q_BLOCK) + q.shape[2:]), 1, 0)
    out = lax.map(fn, qb)
    return jnp.moveaxis(out, 0, 1).reshape((bsz, n) + out.shape[3:])


def diff_attention(q, k, v, lam):
    scale = A_DQK ** -0.5
    def block(qb):
        s = jnp.einsum('bqhmd,bkhmd->bhmqk', qb, k).astype(jnp.float32) * scale
        p = jax.nn.softmax(s, axis=-1)
        w = (p[:, :, 0] - lam * p[:, :, 1]).astype(v.dtype)
        return jnp.einsum('bhqk,bkhd->bqhd', w, v)
    return map_query_blocks(block, q)


def softmax_attention(q, k, v):
    scale = B_DQK ** -0.5
    def block(qb):
        s = jnp.einsum('bqhd,bkhd->bhqk', qb, k).astype(jnp.float32) * scale
        p = jax.nn.softmax(s, axis=-1).astype(v.dtype)
        return jnp.einsum('bhqk,bkhd->bqhd', p, v)
    return map_query_blocks(block, q)


def attn_project(h, p):
    bsz, n, _ = h.shape
    a_q, a_k, a_v, b_qc, b_kvc, b_kr = jnp.split(h @ p['w_in'], ATTN_SPLITS, axis=-1)
    a_q = rmsnorm(a_q.reshape(bsz, n, A_HEADS, 2, A_DQK), p['a_qk_g'][0])
    a_k = rmsnorm(a_k.reshape(bsz, n, A_HEADS, 2, A_DQK), p['a_qk_g'][1])
    a_v = a_v.reshape(bsz, n, A_HEADS, A_DV)
    b_q = (rmsnorm(b_qc, p['b_qa_g']) @ p['b_wq_up']).reshape(bsz, n, B_HEADS, B_DQK)
    b_q = rmsnorm(b_q, p['b_qk_g'][0])
    b_ckv = rmsnorm(b_kvc, p['b_kva_g'])
    return a_q, a_k, a_v, b_q, b_ckv, b_kr


def mla_expand(ckv, krope, p):
    bsz, n, _ = ckv.shape
    kv = (ckv @ p['b_wkv_up']).reshape(bsz, n, B_HEADS, B_NOPE + B_DV)
    k = jnp.concatenate([kv[..., :B_NOPE],
                         jnp.broadcast_to(krope[:, :, None, :], (bsz, n, B_HEADS, B_ROPE))], axis=-1)
    return rmsnorm(k, p['b_qk_g'][1]), kv[..., B_NOPE:]


def rope_tail(x):
    return jnp.concatenate([x[..., :B_NOPE], axial_rope(x[..., B_NOPE:])], axis=-1)


def diff_lambda(p):
    lq1, lk1, lq2, lk2 = p['a_lambda'].astype(jnp.float32)
    return jnp.exp(jnp.sum(lq1 * lk1)) - jnp.exp(jnp.sum(lq2 * lk2)) + p['lam_init']


def attn_merge(o_a, o_b, p):
    bsz, n = o_a.shape[:2]
    o_a = rmsnorm(o_a, p['a_sub_g']) * (1.0 - p['lam_init'])
    o = jnp.concatenate([o_a.reshape(bsz, n, -1), o_b.reshape(bsz, n, -1)], axis=-1)
    return o @ p['w_out']


def attn_context(h, p):
    a_q, a_k, a_v, b_q, b_ckv, b_kr = attn_project(h, p)
    o_a = diff_attention(a_q, a_k, a_v, diff_lambda(p))
    b_k, b_v = mla_expand(b_ckv, b_kr, p)
    o_b = softmax_attention(b_q, b_k, b_v)
    return attn_merge(o_a, o_b, p), a_k, a_v, b_ckv, b_kr


def attn_latent(h, ctx_a_k, ctx_a_v, ctx_ckv, ctx_kr, p):
    a_q, a_k, a_v, b_q, b_ckv, b_kr = attn_project(h, p)
    k_a = jnp.concatenate([ctx_a_k, axial_rope(a_k)], axis=1)
    v_a = jnp.concatenate([ctx_a_v, a_v], axis=1)
    o_a = diff_attention(axial_rope(a_q), k_a, v_a, diff_lambda(p))
    kb_ctx, vb_ctx = mla_expand(ctx_ckv, ctx_kr, p)
    kb_lat, vb_lat = mla_expand(b_ckv, b_kr, p)
    k_b = jnp.concatenate([kb_ctx, rope_tail(kb_lat)], axis=1)
    v_b = jnp.concatenate([vb_ctx, vb_lat], axis=1)
    o_b = softmax_attention(rope_tail(b_q), k_b, v_b)
    return attn_merge(o_a, o_b, p)


def centred_depthwise_conv(x, w, b):
    ch = x.shape[-1]
    width = w.shape[0]
    y = lax.conv_general_dilated(x, w[:, None, :].astype(x.dtype), window_strides=(1,),
                                 padding=[((width - 1) // 2, width // 2)],
                                 dimension_numbers=('NWC', 'WIO', 'NWC'),
                                 feature_group_count=ch)
    return y + b


def ssd_chunk_scan(x, dt, a_head, bmat, cmat, s0):
    bsz, n = x.shape[:2]
    nc, q = n // SSD_CHUNK, SSD_CHUNK
    e = SSD_H // SSD_G
    x = x.astype(jnp.float32).reshape(bsz, nc, q, SSD_G, e, SSD_P)
    dt = dt.reshape(bsz, nc, q, SSD_G, e)
    bmat = bmat.astype(jnp.float32).reshape(bsz, nc, q, SSD_G, SSD_N)
    cmat = cmat.astype(jnp.float32).reshape(bsz, nc, q, SSD_G, SSD_N)
    a_cs = jnp.cumsum(dt * a_head.reshape(SSD_G, e), axis=2)
    seg = a_cs[:, :, :, None] - a_cs[:, :, None, :]
    lower = jnp.tril(jnp.ones((q, q), dtype=bool))[:, :, None, None]
    decay = jnp.exp(jnp.where(lower, seg, -jnp.inf))
    xdt = x * dt[..., None]
    cb = jnp.einsum('bclgn,bcsgn->bclsg', cmat, bmat)
    y_diag = jnp.einsum('bclsge,bcsgep->bclgep', cb[..., None] * decay, xdt)
    decay_end = jnp.exp(a_cs[:, :, -1:] - a_cs)
    chunk_states = jnp.einsum('bclgn,bclge,bclgep->bcgepn', bmat, decay_end, xdt)
    chunk_decay = jnp.exp(a_cs[:, :, -1])

    def step(s, inp):
        st, dec = inp
        return s * dec[..., None, None] + st, s

    s_final, s_in = lax.scan(step, s0.astype(jnp.float32).reshape(bsz, SSD_G, e, SSD_P, SSD_N),
                             (jnp.moveaxis(chunk_states, 1, 0), jnp.moveaxis(chunk_decay, 1, 0)))
    s_in = jnp.moveaxis(s_in, 0, 1)
    y_off = jnp.einsum('bclgn,bcgepn,bclge->bclgep', cmat, s_in, jnp.exp(a_cs))
    y = (y_diag + y_off).reshape(bsz, n, SSD_H, SSD_P)
    return y, s_final.reshape(bsz, SSD_H, SSD_P, SSD_N)


def ssd_mixer(h, s0_f, s0_b, p):
    bsz, n, _ = h.shape
    z, xbc, dt_raw = jnp.split(h @ p['w_in'], SSD_SPLITS, axis=-1)
    xbc = jax.nn.silu(centred_depthwise_conv(xbc, p['conv_w'], p['conv_b']))
    xs, bmat, cmat = jnp.split(xbc, XBC_SPLITS, axis=-1)
    xs = xs.reshape(bsz, n, SSD_H, SSD_P)
    bmat = bmat.reshape(bsz, n, SSD_G, SSD_N)
    cmat = cmat.reshape(bsz, n, SSD_G, SSD_N)
    dt = jax.nn.softplus(dt_raw.astype(jnp.float32).reshape(bsz, n, 2, SSD_H)
                         + p['dt_bias'].astype(jnp.float32))
    a = -jnp.exp(p['a_log'].astype(jnp.float32))
    y_f, s_f = ssd_chunk_scan(xs, dt[:, :, 0], a[0], bmat, cmat, s0_f)
    flip = lambda t: jnp.flip(t, axis=1)
    y_b, s_b = ssd_chunk_scan(flip(xs), flip(dt[:, :, 1]), a[1], flip(bmat), flip(cmat), s0_b)
    y = y_f + flip(y_b) + xs.astype(jnp.float32) * p['d'].astype(jnp.float32)[:, None]
    y = y.reshape(bsz, n, SSD_DI) * jax.nn.silu(z.astype(jnp.float32))
    y = rmsnorm(y.reshape(bsz, n, SSD_G, SSD_DI // SSD_G)).reshape(bsz, n, SSD_DI)
    y = y * p['norm_g'].astype(jnp.float32)
    return y.astype(h.dtype) @ p['w_out'], jnp.stack([s_f, s_b], axis=1)


def swiglu(u):
    return jax.nn.silu(u[..., :D_EXPERT]) * u[..., D_EXPERT:]


def grouped_moe(h, router_w, router_bias, w_in, w_out, sh_in, sh_out):
    bsz, n, d = h.shape
    t = h.reshape(-1, d)
    scores = jax.nn.sigmoid((t @ router_w).astype(jnp.float32))
    sel = scores + router_bias.astype(jnp.float32)
    grp_score = lax.top_k(sel.reshape(-1, N_EXPERT_GROUPS, EXPERTS_PER_GROUP), TOP_K)[0].sum(-1)
    best = jnp.argmax(grp_score, axis=-1)
    in_group = jnp.repeat(jnp.arange(N_EXPERT_GROUPS)[None, :] == best[:, None], EXPERTS_PER_GROUP, axis=-1)
    _, idx = lax.top_k(jnp.where(in_group, sel, -jnp.inf), TOP_K)
    w = jnp.take_along_axis(scores, idx, axis=-1)
    w = w / jnp.sum(w, axis=-1, keepdims=True)
    gates = jnp.sum(jax.nn.one_hot(idx, N_EXPERTS, dtype=jnp.float32) * w[..., None], axis=1)
    hid = swiglu(jnp.einsum('td,edf->tef', t, w_in))
    routed = jnp.einsum('tef,efd->td', hid * gates[..., None].astype(hid.dtype), w_out)
    shared = swiglu(t @ sh_in) @ sh_out
    return (routed + shared).reshape(bsz, n, d)


def setup_inputs(seed: int = 0) -> dict:
    key = jax.random.key(seed)
    ks = iter(jax.random.split(key, 64))
    def nrm(shape, scale=1.0):
        return jax.random.normal(next(ks), shape, jnp.float32) * scale
    def gain(shape):
        return 1.0 + nrm(shape, 0.02)
    dt0 = jnp.exp(jax.random.uniform(next(ks), (N_SSD, 2, SSD_H), jnp.float32,
                                     minval=math.log(1e-3), maxval=math.log(1e-1)))
    return {
        'x_prompt': nrm((BATCH, SEQ, D_MODEL)),
        'x_sample': nrm((DEC_BATCH, DEC_SEQ, D_MODEL)),
        'cache_a_k': nrm((DEC_BATCH, N_ATTN, PAST_LEN, A_HEADS, 2, A_DQK)),
        'cache_a_v': nrm((DEC_BATCH, N_ATTN, PAST_LEN, A_HEADS, A_DV)),
        'cache_b_ckv': nrm((DEC_BATCH, N_ATTN, PAST_LEN, B_KV_LORA)),
        'cache_b_krope': nrm((DEC_BATCH, N_ATTN, PAST_LEN, B_ROPE)),
        'state_ssd': nrm((DEC_BATCH, N_SSD, 2, SSD_H, SSD_P, SSD_N), 0.5),
        'c': nrm((DEC_BATCH, D_MODEL)),
        'c_ctx': nrm((D_MODEL,)),
        'mod_w': nrm((DEPTH, D_MODEL, 6 * D_MODEL), 0.5 * D_MODEL ** -0.5),
        'mod_b': nrm((DEPTH, 6 * D_MODEL), 0.02),
        'norm1_g': gain((DEPTH, D_MODEL)),
        'norm2_g': gain((DEPTH, D_MODEL)),
        'attn_w_in': nrm((N_ATTN, D_MODEL, ATTN_IN), D_MODEL ** -0.5),
        'a_qk_g': gain((N_ATTN, 2, A_DQK)),
        'a_lambda': nrm((N_ATTN, 4, A_DQK), 0.1),
        'a_sub_g': gain((N_ATTN, A_DV)),
        'b_qa_g': gain((N_ATTN, B_Q_LORA)),
        'b_wq_up': nrm((N_ATTN, B_Q_LORA, B_HEADS * B_DQK), B_Q_LORA ** -0.5),
        'b_kva_g': gain((N_ATTN, B_KV_LORA)),
        'b_wkv_up': nrm((N_ATTN, B_KV_LORA, B_HEADS * (B_NOPE + B_DV)), B_KV_LORA ** -0.5),
        'b_qk_g': gain((N_ATTN, 2, B_DQK)),
        'attn_w_out': nrm((N_ATTN, ATTN_OUT, D_MODEL), ATTN_OUT ** -0.5),
        'ssd_w_in': nrm((N_SSD, D_MODEL, SSD_IN), D_MODEL ** -0.5),
        'ssd_conv_w': nrm((N_SSD, SSD_CONV, SSD_CONV_DIM), SSD_CONV ** -0.5),
        'ssd_conv_b': nrm((N_SSD, SSD_CONV_DIM), 0.02),
        'ssd_dt_bias': dt0 + jnp.log(-jnp.expm1(-dt0)),
        'ssd_a_log': jnp.log(jax.random.uniform(next(ks), (N_SSD, 2, SSD_H), jnp.float32, minval=1.0, maxval=16.0)),
        'ssd_d': gain((N_SSD, SSD_H)),
        'ssd_norm_g': gain((N_SSD, SSD_DI)),
        'ssd_w_out': nrm((N_SSD, SSD_DI, D_MODEL), SSD_DI ** -0.5),
        'router_w': nrm((D_MODEL, N_EXPERTS), D_MODEL ** -0.5),
        'router_bias': nrm((N_EXPERTS,), 0.01),
        'moe_w_in': nrm((DEPTH, N_EXPERTS, D_MODEL, 2 * D_EXPERT), D_MODEL ** -0.5),
        'moe_w_out': nrm((DEPTH, N_EXPERTS, D_EXPERT, D_MODEL), D_EXPERT ** -0.5),
        'shared_w_in': nrm((DEPTH, D_MODEL, 2 * D_EXPERT), D_MODEL ** -0.5),
        'shared_w_out': nrm((DEPTH, D_EXPERT, D_MODEL), D_EXPERT ** -0.5),
    }


def reference(x_prompt, x_sample, cache_a_k, cache_a_v, cache_b_ckv, cache_b_krope, state_ssd,
              c, c_ctx, mod_w, mod_b, norm1_g, norm2_g, attn_w_in, a_qk_g, a_lambda, a_sub_g,
              b_qa_g, b_wq_up, b_kva_g, b_wkv_up, b_qk_g, attn_w_out, ssd_w_in, ssd_conv_w,
              ssd_conv_b, ssd_dt_bias, ssd_a_log, ssd_d, ssd_norm_g, ssd_w_out, router_w,
              router_bias, moe_w_in, moe_w_out, shared_w_in, shared_w_out):
    xp, xs = x_prompt, x_sample
    silu_ctx = jax.nn.silu(c_ctx)
    silu_c = jax.nn.silu(c)
    new_a_k, new_a_v, new_ckv, new_kr, new_ssd = [], [], [], [], []
    for l in range(DEPTH):
        i = l // 2
        mp = jnp.split(silu_ctx @ mod_w[l] + mod_b[l], 6, axis=-1)
        ms = jnp.split((silu_c @ mod_w[l] + mod_b[l])[:, None, :], 6, axis=-1)
        hp = modulate(xp, norm1_g[l], mp[0], mp[1])
        hs = modulate(xs, norm1_g[l], ms[0], ms[1])
        if l % 2 == 0:
            ap = {'w_in': attn_w_in[i], 'a_qk_g': a_qk_g[i], 'a_lambda': a_lambda[i],
                  'a_sub_g': a_sub_g[i], 'b_qa_g': b_qa_g[i], 'b_wq_up': b_wq_up[i],
                  'b_kva_g': b_kva_g[i], 'b_wkv_up': b_wkv_up[i], 'b_qk_g': b_qk_g[i],
                  'w_out': attn_w_out[i], 'lam_init': 0.8 - 0.6 * math.exp(-0.3 * l)}
            out_p, ak, av, ckv, kr = attn_context(hp, ap)
            out_s = attn_latent(hs, cache_a_k[:, i], cache_a_v[:, i], cache_b_ckv[:, i],
                                cache_b_krope[:, i], ap)
            new_a_k.append(ak)
            new_a_v.append(av)
            new_ckv.append(ckv)
            new_kr.append(kr)
        else:
            sp = {'w_in': ssd_w_in[i], 'conv_w': ssd_conv_w[i], 'conv_b': ssd_conv_b[i],
                  'dt_bias': ssd_dt_bias[i], 'a_log': ssd_a_log[i], 'd': ssd_d[i],
                  'norm_g': ssd_norm_g[i], 'w_out': ssd_w_out[i]}
            zero = jnp.zeros((xp.shape[0], SSD_H, SSD_P, SSD_N), jnp.float32)
            out_p, st = ssd_mixer(hp, zero, zero, sp)
            out_s, _ = ssd_mixer(hs, state_ssd[:, i, 0], state_ssd[:, i, 1], sp)
            new_ssd.append(st)
        xp = xp + mp[2] * out_p
        xs = xs + ms[2] * out_s
        hp = modulate(xp, norm2_g[l], mp[3], mp[4])
        hs = modulate(xs, norm2_g[l], ms[3], ms[4])
        xp = xp + mp[5] * grouped_moe(hp, router_w, router_bias, moe_w_in[l], moe_w_out[l],
                                      shared_w_in[l], shared_w_out[l])
        xs = xs + ms[5] * grouped_moe(hs, router_w, router_bias, moe_w_in[l], moe_w_out[l],
                                      shared_w_in[l], shared_w_out[l])
    return (xp, xs, jnp.stack(new_a_k, axis=1), jnp.stack(new_a_v, axis=1),
            jnp.stack(new_ckv, axis=1), jnp.stack(new_kr, axis=1), jnp.stack(new_ssd, axis=1))
```

```python
import functools
import math

import jax
import jax.numpy as jnp
from jax import lax
from jax.experimental import pallas as pl
from jax.experimental.pallas import tpu as pltpu

F32 = jnp.float32
BF16 = jnp.bfloat16

D = 1024
EPS = 1e-6
GRID_W = 64
ROPE_BASE = 10000.0
LANE = 128

A_HEADS = 4
A_DQK = 64
A_DV = 128
B_HEADS = 4
B_NOPE = 128
B_ROPE = 64
B_DQK = B_NOPE + B_ROPE
B_DV = 128
B_Q_LORA = 384
B_KV_LORA = 256

SSD_DI = 2048
SSD_P = 64
SSD_H = 32
SSD_N = 128
SSD_G = 4
SSD_CHUNK = 128
SSD_CONV_DIM = SSD_DI + 2 * SSD_G * SSD_N

N_EXPERTS = 16
N_GROUPS = 4
EPG = 4
D_EXPERT = 256

VMEM_LIMIT = 56 * 1024 * 1024


def _cparams(*sem):
    return pltpu.CompilerParams(dimension_semantics=sem, vmem_limit_bytes=VMEM_LIMIT)


def _dot(a, b):
    return jnp.dot(a, b, preferred_element_type=F32)


def _dot_nt(a, b):
    return lax.dot_general(a, b, (((1,), (1,)), ((), ())), preferred_element_type=F32)


def _split3(x):
    x1 = x.astype(BF16)
    r = x - x1.astype(F32)
    x2 = r.astype(BF16)
    x3 = (r - x2.astype(F32)).astype(BF16)
    return x1, x2, x3


def _silu(x):
    return x * (1.0 / (1.0 + jnp.exp(-x)))


def _rms(x, n):
    return lax.rsqrt(jnp.sum(x * x, axis=-1, keepdims=True) * (1.0 / n) + EPS)


def _modulated(x, g, shift, scale):
    return x * _rms(x, D) * g * (1.0 + scale) + shift


def _mod_kernel(c_ref, w_ref, b_ref, o_ref):
    c = _silu(c_ref[...])
    w = w_ref[0]
    c1, c2, c3 = _split3(c)
    w1, w2, w3 = _split3(w)
    acc = _dot(c1, w1) + (_dot(c1, w2) + _dot(c2, w1)) + (_dot(c2, w2) + _dot(c1, w3) + _dot(c3, w1))
    o_ref[0] = acc + b_ref[0]


def _mod_vectors(c_all, mod_w, mod_b):
    depth = mod_w.shape[0]
    rows = c_all.shape[0]
    tn = 1536
    return pl.pallas_call(
        _mod_kernel,
        out_shape=jax.ShapeDtypeStruct((depth, rows, 6 * D), F32),
        grid=(depth, 6 * D // tn),
        in_specs=[
            pl.BlockSpec((rows, D), lambda l, j: (0, 0)),
            pl.BlockSpec((1, D, tn), lambda l, j: (l, 0, j)),
            pl.BlockSpec((1, 1, tn), lambda l, j: (l, 0, j)),
        ],
        out_specs=pl.BlockSpec((1, rows, tn), lambda l, j: (l, 0, j)),
        compiler_params=_cparams("parallel", "parallel"),
        name="mod_vectors",
    )(c_all, mod_w, mod_b.reshape(depth, 1, 6 * D))


def _rope_tables(n_tok):
    t = jnp.arange(n_tok)
    row = (t // GRID_W).astype(F32)
    col = (t % GRID_W).astype(F32)
    half = B_ROPE // 2
    inv = ROPE_BASE ** (-2.0 * jnp.arange(half // 2, dtype=F32) / half)
    ang = jnp.concatenate([row[:, None] * inv, col[:, None] * inv], axis=-1)
    cos, sin = jnp.cos(ang), jnp.sin(ang)
    zero = jnp.zeros((n_tok, LANE - 2 * half), F32)
    cos_t = jnp.concatenate([cos, cos, zero], axis=-1)
    sin_t = jnp.concatenate([-sin, sin, zero], axis=-1)
    return cos_t, sin_t


def _rope_chunk(c, cos, sin):
    lane = lax.broadcasted_iota(jnp.int32, c.shape, 1)
    swapped = jnp.where(lane < 32, pltpu.roll(c, LANE - 32, 1), pltpu.roll(c, 32, 1))
    return c * cos + swapped * sin


N_ATTN_COLS = 3328
_AQ0, _AK0, _AV0, _BQ0, _BKV0, _BKR0 = 0, 1024, 2048, 2560, 2944, 3200


def _attn_in_kernel(x_ref, mod_ref, g1_ref, w_ref, gq_ref, gk_ref, gqa_ref, wq_ref, gbq_ref, gkva_ref,
                    cos_ref, sin_ref, qa_ref, ka_ref, va_ref, qb_ref, ckv_ref, kr_ref, *, rope):
    x = x_ref[...]
    h = _modulated(x, g1_ref[...], mod_ref[0, 0:1, :], mod_ref[0, 1:2, :])
    z = _dot(h.astype(BF16), w_ref[...])
    if rope:
        cos = cos_ref[...]
        sin = sin_ref[...]
    for j in range(2 * A_HEADS):
        sl = slice(j * LANE, (j + 1) * LANE)
        q = z[:, _AQ0 + j * LANE:_AQ0 + (j + 1) * LANE]
        q = q * _rms(q, A_DQK) * gq_ref[:, sl]
        k = z[:, _AK0 + j * LANE:_AK0 + (j + 1) * LANE]
        k = k * _rms(k, A_DQK) * gk_ref[:, sl]
        if rope:
            q = _rope_chunk(q, cos, sin)
            k = _rope_chunk(k, cos, sin)
        qa_ref[:, sl] = q.astype(qa_ref.dtype)
        ka_ref[:, sl] = k.astype(ka_ref.dtype)
    va_ref[...] = z[:, _AV0:_AV0 + A_HEADS * A_DV].astype(va_ref.dtype)
    qc = z[:, _BQ0:_BQ0 + B_Q_LORA]
    qc = qc * _rms(qc, B_Q_LORA) * gqa_ref[...]
    qb = _dot(qc.astype(BF16), wq_ref[...])
    for hh in range(B_HEADS):
        lo = qb[:, hh * 256:hh * 256 + LANE]
        hi = qb[:, hh * 256 + LANE:(hh + 1) * 256]
        inv = lax.rsqrt((jnp.sum(lo * lo, axis=-1, keepdims=True)
                         + jnp.sum(hi * hi, axis=-1, keepdims=True)) * (1.0 / B_DQK) + EPS)
        lo = lo * inv * gbq_ref[:, hh * 256:hh * 256 + LANE]
        hi = hi * inv * gbq_ref[:, hh * 256 + LANE:(hh + 1) * 256]
        if rope:
            hi = _rope_chunk(hi, cos, sin)
        qb_ref[:, hh * 256:hh * 256 + LANE] = lo.astype(qb_ref.dtype)
        qb_ref[:, hh * 256 + LANE:(hh + 1) * 256] = hi.astype(qb_ref.dtype)
    kvc = z[:, _BKV0:_BKV0 + B_KV_LORA]
    ckv_ref[...] = kvc * _rms(kvc, B_KV_LORA) * gkva_ref[...]
    kr_ref[...] = z[:, _BKR0:_BKR0 + LANE]


def _attn_in(x, mod, g1, wp, rope_tabs, *, seq, per_batch_mod, rope, kv_dtype, tm):
    t = x.shape[0]
    steps_per_seq = seq // tm
    mod_map = (lambda i: (i // steps_per_seq, 0, 0)) if per_batch_mod else (lambda i: (0, 0, 0))
    tab_map = lambda i: (i % steps_per_seq, 0)
    row = lambda n: pl.BlockSpec((tm, n), lambda i: (i, 0))
    full = lambda a: pl.BlockSpec(a.shape, lambda i: (0,) * a.ndim)
    cos_t, sin_t = rope_tabs
    ins = [x, mod, g1, wp["w_in"], wp["gq"], wp["gk"], wp["gqa"], wp["wq"], wp["gbq"], wp["gkva"], cos_t, sin_t]
    in_specs = [row(D), pl.BlockSpec((1, 6, D), mod_map), full(g1), full(wp["w_in"]), full(wp["gq"]),
                full(wp["gk"]), full(wp["gqa"]), full(wp["wq"]), full(wp["gbq"]), full(wp["gkva"]),
                pl.BlockSpec((tm, LANE), tab_map), pl.BlockSpec((tm, LANE), tab_map)]
    out_shape = [
        jax.ShapeDtypeStruct((t, 1024), BF16),
        jax.ShapeDtypeStruct((t, 1024), kv_dtype),
        jax.ShapeDtypeStruct((t, 512), kv_dtype),
        jax.ShapeDtypeStruct((t, 1024), BF16),
        jax.ShapeDtypeStruct((t, B_KV_LORA), F32),
        jax.ShapeDtypeStruct((t, LANE), F32),
    ]
    out_specs = [row(1024), row(1024), row(512), row(1024), row(B_KV_LORA), row(LANE)]
    return pl.pallas_call(
        functools.partial(_attn_in_kernel, rope=rope),
        out_shape=out_shape, grid=(t // tm,), in_specs=in_specs, out_specs=out_specs,
        compiler_params=_cparams("parallel"), name="attn_in",
    )(*ins)


def _mla_expand_kernel(ckv_ref, kr_ref, w_ref, g_ref, cos_ref, sin_ref, kb_ref, vb_ref, *, rope):
    kv = _dot(ckv_ref[...].astype(BF16), w_ref[...])
    kr = kr_ref[...]
    kr_ss = jnp.sum(kr * kr, axis=-1, keepdims=True)
    for hh in range(B_HEADS):
        kn = kv[:, hh * 256:hh * 256 + LANE]
        inv = lax.rsqrt((jnp.sum(kn * kn, axis=-1, keepdims=True) + kr_ss) * (1.0 / B_DQK) + EPS)
        hi = kr * inv * g_ref[:, LANE:]
        if rope:
            hi = _rope_chunk(hi, cos_ref[...], sin_ref[...])
        kb_ref[:, hh * 256:hh * 256 + LANE] = (kn * inv * g_ref[:, :LANE]).astype(kb_ref.dtype)
        kb_ref[:, hh * 256 + LANE:(hh + 1) * 256] = hi.astype(kb_ref.dtype)
        vb_ref[:, hh * LANE:(hh + 1) * LANE] = kv[:, hh * 256 + LANE:(hh + 1) * 256].astype(vb_ref.dtype)


def _mla_expand(ckv, kr, wkv, gk, rope_tabs, *, seq, rope, tm):
    t = ckv.shape[0]
    steps_per_seq = seq // tm
    tab_map = lambda i: (i % steps_per_seq, 0)
    row = lambda n: pl.BlockSpec((tm, n), lambda i: (i, 0))
    full = lambda a: pl.BlockSpec(a.shape, lambda i: (0,) * a.ndim)
    cos_t, sin_t = rope_tabs
    return pl.pallas_call(
        functools.partial(_mla_expand_kernel, rope=rope),
        out_shape=[jax.ShapeDtypeStruct((t, 1024), BF16), jax.ShapeDtypeStruct((t, 512), BF16)],
        grid=(t // tm,),
        in_specs=[row(B_KV_LORA), row(LANE), full(wkv), full(gk),
                  pl.BlockSpec((tm, LANE), tab_map), pl.BlockSpec((tm, LANE), tab_map)],
        out_specs=[row(1024), row(512)],
        compiler_params=_cparams("parallel"), name="mla_expand",
    )(ckv, kr, wkv, gk, cos_t, sin_t)


def _softmax_pieces(q, keys):
    s = [_dot_nt(q, k) for k in keys]
    mx = functools.reduce(jnp.maximum, [jnp.max(si, axis=-1, keepdims=True) for si in s])
    p = [jnp.exp(si - mx) for si in s]
    den = functools.reduce(jnp.add, [jnp.sum(pi, axis=-1, keepdims=True) for pi in p])
    return p, 1.0 / den


def _attn_kernel(*refs, n_pieces, lam_init):
    qa_ref, qb_ref, lam_ref, gsub_ref = refs[:4]
    kv_refs = refs[4:4 + 4 * n_pieces]
    o_ref = refs[4 + 4 * n_pieces]
    ka = kv_refs[0::4]
    va = kv_refs[1::4]
    kb = kv_refs[2::4]
    vb = kv_refs[3::4]
    lam_p = lam_ref[...]
    lam = (jnp.exp(jnp.sum(lam_p[0:1] * lam_p[1:2], axis=-1, keepdims=True))
           - jnp.exp(jnp.sum(lam_p[2:3] * lam_p[3:4], axis=-1, keepdims=True)) + lam_init)
    for h in range(A_HEADS):
        ws = []
        for m in range(2):
            sl = slice((2 * h + m) * LANE, (2 * h + m + 1) * LANE)
            p, rden = _softmax_pieces(qa_ref[0, :, sl], [k[0, :, sl].astype(BF16) for k in ka])
            ws.append([pi * rden for pi in p])
        o = None
        for i in range(n_pieces):
            w = (ws[0][i] - lam * ws[1][i]).astype(BF16)
            oi = _dot(w, va[i][0, :, h * A_DV:(h + 1) * A_DV].astype(BF16))
            o = oi if o is None else o + oi
        o = o * _rms(o, A_DV) * gsub_ref[...] * (1.0 - lam_init)
        o_ref[0, :, h * A_DV:(h + 1) * A_DV] = o.astype(o_ref.dtype)
    for h in range(B_HEADS):
        sl = slice(h * 256, (h + 1) * 256)
        p, rden = _softmax_pieces(qb_ref[0, :, sl], [k[0, :, sl] for k in kb])
        o = None
        for i in range(n_pieces):
            oi = _dot(p[i].astype(BF16), vb[i][0, :, h * B_DV:(h + 1) * B_DV])
            o = oi if o is None else o + oi
        o = o * rden
        c0 = A_HEADS * A_DV + h * B_DV
        o_ref[0, :, c0:c0 + B_DV] = o.astype(o_ref.dtype)


def _attention(qa, qb, lam_p, gsub, pieces, *, lam_init, tq):
    b, s, _ = qa.shape
    qspec = pl.BlockSpec((1, tq, 1024), lambda bi, qi: (bi, qi, 0))
    full = lambda a: pl.BlockSpec(a.shape, lambda bi, qi: (0,) * a.ndim)
    ins = [qa, qb, lam_p, gsub]
    in_specs = [qspec, qspec, full(lam_p), full(gsub)]
    for piece in pieces:
        for a in piece:
            ins.append(a)
            in_specs.append(pl.BlockSpec((1,) + a.shape[1:], lambda bi, qi: (bi, 0, 0)))
    return pl.pallas_call(
        functools.partial(_attn_kernel, n_pieces=len(pieces), lam_init=lam_init),
        out_shape=jax.ShapeDtypeStruct((b, s, 1024), BF16),
        grid=(b, s // tq), in_specs=in_specs,
        out_specs=pl.BlockSpec((1, tq, 1024), lambda bi, qi: (bi, qi, 0)),
        compiler_params=_cparams("parallel", "parallel"), name="attn_core",
    )(*ins)


def _proj_res_kernel(o_ref, x_ref, mod_ref, w_ref, y_ref, *, gate_row):
    y_ref[...] = x_ref[...] + mod_ref[0, gate_row:gate_row + 1, :] * _dot(o_ref[...], w_ref[...])


def _proj_res(o, x, mod, w, *, seq, per_batch_mod, gate_row, tm):
    t, k = o.shape
    steps_per_seq = seq // tm
    mod_map = (lambda i: (i // steps_per_seq, 0, 0)) if per_batch_mod else (lambda i: (0, 0, 0))
    return pl.pallas_call(
        functools.partial(_proj_res_kernel, gate_row=gate_row),
        out_shape=jax.ShapeDtypeStruct((t, D), F32), grid=(t // tm,),
        in_specs=[pl.BlockSpec((tm, k), lambda i: (i, 0)), pl.BlockSpec((tm, D), lambda i: (i, 0)),
                  pl.BlockSpec((1, 6, D), mod_map), pl.BlockSpec(w.shape, lambda i: (0, 0))],
        out_specs=pl.BlockSpec((tm, D), lambda i: (i, 0)),
        compiler_params=_cparams("parallel"), name="proj_res",
    )(o, x, mod, w)


def _route(logits_t, bias_col):
    tm = logits_t.shape[1]
    scores = 1.0 / (1.0 + jnp.exp(-logits_t[:N_EXPERTS]))
    sel = scores + bias_col[:N_EXPERTS]
    grp = []
    for g in range(N_GROUPS):
        r = [sel[EPG * g + j:EPG * g + j + 1] for j in range(EPG)]
        best2 = None
        for a in range(EPG):
            for b2 in range(a + 1, EPG):
                pair = r[a] + r[b2]
                best2 = pair if best2 is None else jnp.maximum(best2, pair)
        grp.append(best2)
    best = jnp.zeros((1, tm), jnp.int32)
    best_v = grp[0]
    for g in range(1, N_GROUPS):
        better = grp[g] > best_v
        best = jnp.where(better, g, best)
        best_v = jnp.where(better, grp[g], best_v)
    v = []
    sc = []
    for j in range(EPG):
        vj = sel[j:j + 1]
        sj = scores[j:j + 1]
        for g in range(1, N_GROUPS):
            vj = jnp.where(best == g, sel[EPG * g + j:EPG * g + j + 1], vj)
            sj = jnp.where(best == g, scores[EPG * g + j:EPG * g + j + 1], sj)
        v.append(vj)
        sc.append(sj)
    picked = []
    for j in range(EPG):
        rank = jnp.zeros((1, tm), jnp.int32)
        for k in range(EPG):
            if k == j:
                continue
            ahead = (v[k] >= v[j]) if k < j else (v[k] > v[j])
            rank = rank + ahead.astype(jnp.int32)
        picked.append(rank < 2)
    den = functools.reduce(jnp.add, [jnp.where(picked[j], sc[j], 0.0) for j in range(EPG)])
    rows = []
    for e in range(N_EXPERTS):
        g, j = divmod(e, EPG)
        rows.append(jnp.where((best == g) & picked[j], sc[j] / den, 0.0))
    rows.append(jnp.ones((1, tm), F32))
    rows.append(jnp.zeros((LANE - N_EXPERTS - 1, tm), F32))
    return jnp.concatenate(rows, axis=0)


def _moe_dense_kernel(x_ref, mod_ref, g2_ref, rw_ref, rb_ref, win_ref, wout_ref, y_ref, h_scr, g_scr, acc_scr):
    e = pl.program_id(1)

    @pl.when(e == 0)
    def _():
        x = x_ref[...]
        h = _modulated(x, g2_ref[...], mod_ref[0, 3:4, :], mod_ref[0, 4:5, :])
        hb = h.astype(BF16)
        h_scr[...] = hb
        h1, h2, h3 = _split3(h)
        w1, w2, w3 = _split3(rw_ref[...])
        logits_t = (_dot_nt(w1, h1) + (_dot_nt(w1, h2) + _dot_nt(w2, h1))
                    + (_dot_nt(w2, h2) + _dot_nt(w1, h3) + _dot_nt(w3, h1)))
        g_scr[...] = jnp.transpose(_route(logits_t, rb_ref[...]))
        acc_scr[...] = jnp.zeros_like(acc_scr)

    lane = lax.broadcasted_iota(jnp.int32, g_scr.shape, 1)
    gate = jnp.sum(jnp.where(lane == e, g_scr[...], 0.0), axis=-1, keepdims=True)
    u = _dot(h_scr[...], win_ref[0])
    hid = _silu(u[:, :D_EXPERT]) * u[:, D_EXPERT:] * gate
    acc_scr[...] += _dot(hid.astype(BF16), wout_ref[0])

    @pl.when(e == pl.num_programs(1) - 1)
    def _():
        y_ref[...] = x_ref[...] + mod_ref[0, 5:6, :] * acc_scr[...]


def _moe_dense(x, mod, g2, rw_t, rb_col, win_all, wout_all, *, seq, per_batch_mod, tm):
    t = x.shape[0]
    n_e = win_all.shape[0]
    steps_per_seq = seq // tm
    mod_map = (lambda i, e: (i // steps_per_seq, 0, 0)) if per_batch_mod else (lambda i, e: (0, 0, 0))
    return pl.pallas_call(
        _moe_dense_kernel,
        out_shape=jax.ShapeDtypeStruct((t, D), F32), grid=(t // tm, n_e),
        in_specs=[pl.BlockSpec((tm, D), lambda i, e: (i, 0)), pl.BlockSpec((1, 6, D), mod_map),
                  pl.BlockSpec(g2.shape, lambda i, e: (0, 0)), pl.BlockSpec(rw_t.shape, lambda i, e: (0, 0)),
                  pl.BlockSpec(rb_col.shape, lambda i, e: (0, 0)),
                  pl.BlockSpec((1, D, 2 * D_EXPERT), lambda i, e: (e, 0, 0)),
                  pl.BlockSpec((1, D_EXPERT, D), lambda i, e: (e, 0, 0))],
        out_specs=pl.BlockSpec((tm, D), lambda i, e: (i, 0)),
        scratch_shapes=[pltpu.VMEM((tm, D), BF16), pltpu.VMEM((tm, LANE), F32), pltpu.VMEM((tm, D), F32)],
        compiler_params=_cparams("parallel", "arbitrary"), name="moe_dense",
    )(x, mod, g2, rw_t, rb_col, win_all, wout_all)


SSD_IN_COLS = SSD_DI + SSD_CONV_DIM + 2 * LANE


def _ssd_in_kernel(x_ref, mod_ref, g1_ref, w_ref, z_ref, xbc_ref, dt_ref):
    h = _modulated(x_ref[...], g1_ref[...], mod_ref[0, 0:1, :], mod_ref[0, 1:2, :])
    hb = h.astype(BF16)
    z_ref[...] = _dot(hb, w_ref[:, :SSD_DI]).astype(z_ref.dtype)
    xbc_ref[...] = _dot(hb, w_ref[:, SSD_DI:SSD_DI + SSD_CONV_DIM]).astype(xbc_ref.dtype)
    dt_ref[...] = _dot(hb, w_ref[:, SSD_DI + SSD_CONV_DIM:])


def _ssd_in(x, mod, g1, w, *, seq, per_batch_mod, tm):
    t = x.shape[0]
    steps_per_seq = seq // tm
    mod_map = (lambda i: (i // steps_per_seq, 0, 0)) if per_batch_mod else (lambda i: (0, 0, 0))
    row = lambda n: pl.BlockSpec((tm, n), lambda i: (i, 0))
    return pl.pallas_call(
        _ssd_in_kernel,
        out_shape=[jax.ShapeDtypeStruct((t, SSD_DI), F32), jax.ShapeDtypeStruct((t, SSD_CONV_DIM), F32),
                   jax.ShapeDtypeStruct((t, 2 * LANE), F32)],
        grid=(t // tm,),
        in_specs=[row(D), pl.BlockSpec((1, 6, D), mod_map), pl.BlockSpec(g1.shape, lambda i: (0, 0)),
                  pl.BlockSpec(w.shape, lambda i: (0, 0))],
        out_specs=[row(SSD_DI), row(SSD_CONV_DIM), row(2 * LANE)],
        compiler_params=_cparams("parallel"), name="ssd_in",
    )(x, mod, g1, w)


def _conv_kernel(x_ref, w_ref, b_ref, o_ref):
    x = x_ref[0]
    s = x.shape[0]
    rowi = lax.broadcasted_iota(jnp.int32, x.shape, 0)
    prev = jnp.where(rowi == 0, 0.0, pltpu.roll(x, 1, 0))
    nxt = jnp.where(rowi == s - 1, 0.0, pltpu.roll(x, s - 1, 0))
    y = prev * w_ref[0:1, :] + x * w_ref[1:2, :] + nxt * w_ref[2:3, :] + b_ref[...]
    o_ref[0] = _silu(y).astype(o_ref.dtype)


def _conv_silu(xbc, w, b, *, cw):
    bsz, s, c = xbc.shape
    return pl.pallas_call(
        _conv_kernel,
        out_shape=jax.ShapeDtypeStruct((bsz, s, c), BF16), grid=(bsz, c // cw),
        in_specs=[pl.BlockSpec((1, s, cw), lambda bi, j: (bi, 0, j)), pl.BlockSpec((3, cw), lambda bi, j: (0, j)),
                  pl.BlockSpec((1, cw), lambda bi, j: (0, j))],
        out_specs=pl.BlockSpec((1, s, cw), lambda bi, j: (bi, 0, j)),
        compiler_params=_cparams("parallel", "parallel"), name="conv_silu",
    )(xbc, w, b)


def _expand_heads(cols, j):
    lane = lax.broadcasted_iota(jnp.int32, (SSD_CHUNK, LANE), 1)
    return jnp.where(lane < SSD_P, cols[:, 2 * j:2 * j + 1], cols[:, 2 * j + 1:2 * j + 2])


def _ssd_scan_kernel(*refs, reverse, add_skip):
    if add_skip:
        x_ref, b_ref, c_ref, dt_ref, bias_ref, alog_ref, s0_ref, d_ref, yin_ref, y_ref, sfin_ref, state = refs
    else:
        x_ref, b_ref, c_ref, dt_ref, bias_ref, alog_ref, s0_ref, y_ref, sfin_ref, state = refs
    ci = pl.program_id(1)
    q = SSD_CHUNK

    @pl.when(ci == 0)
    def _():
        state[...] = s0_ref[0]

    dtr = dt_ref[0] + bias_ref[...]
    dt = jnp.maximum(dtr, 0.0) + jnp.log(1.0 + jnp.exp(-jnp.abs(dtr)))
    a = dt * (-jnp.exp(alog_ref[...]))
    li = lax.broadcasted_iota(jnp.int32, (q, q), 0)
    si = lax.broadcasted_iota(jnp.int32, (q, q), 1)
    seen = (si >= li) if reverse else (si <= li)
    tri = jnp.where(seen, 1.0, 0.0).astype(BF16)
    a1, a2, a3 = _split3(a)
    cs = _dot(tri, a1) + _dot(tri, a2) + _dot(tri, a3)
    cs_t = jnp.transpose(cs)
    last = 0 if reverse else q - 1
    tot = cs[last:last + 1, :]
    e_in = jnp.exp(cs)
    e_end = jnp.exp(tot - cs)
    x = x_ref[0].astype(F32)
    for g in range(SSD_G):
        bg = b_ref[0, :, g * SSD_N:(g + 1) * SSD_N]
        cg = c_ref[0, :, g * SSD_N:(g + 1) * SSD_N]
        cb = _dot_nt(cg, bg)
        s_g = state[g * 512:(g + 1) * 512, :]
        y_off = _dot_nt(cg, s_g.astype(BF16))
        xw_tiles = []
        for jj in range(4):
            j = g * 4 + jj
            xt = x[:, j * LANE:(j + 1) * LANE]
            xdt = xt * _expand_heads(dt, j)
            lane = lax.broadcasted_iota(jnp.int32, (q, LANE), 1)
            yt = y_off[:, jj * LANE:(jj + 1) * LANE] * _expand_heads(e_in, j)
            for half in range(2):
                h = 2 * j + half
                seg = cs[:, h:h + 1] - cs_t[h:h + 1, :]
                m = (cb * jnp.where(seen, jnp.exp(seg), 0.0)).astype(BF16)
                xh = jnp.where((lane >= SSD_P) == bool(half), xdt, 0.0).astype(BF16)
                yt = yt + _dot(m, xh)
            if add_skip:
                yt = yt + yin_ref[0, :, j * LANE:(j + 1) * LANE] + xt * d_ref[:, j * LANE:(j + 1) * LANE]
            y_ref[0, :, j * LANE:(j + 1) * LANE] = yt.astype(y_ref.dtype)
            xw_tiles.append(xdt * _expand_heads(e_end, j))
        xw = jnp.concatenate(xw_tiles, axis=1)
        upd = _dot(jnp.transpose(xw).astype(BF16), bg)
        for e in range(8):
            h = g * 8 + e
            dec = jnp.exp(cs_t[h:h + 1, last:last + 1])
            r0 = g * 512 + e * SSD_P
            state[r0:r0 + SSD_P, :] = s_g[e * SSD_P:(e + 1) * SSD_P, :] * dec + upd[e * SSD_P:(e + 1) * SSD_P, :]

    @pl.when(ci == pl.num_programs(1) - 1)
    def _():
        sfin_ref[0] = state[...]


def _ssd_scan(xbc, dt, bias, alog, s0, skip=None, *, reverse):
    bsz, s, _ = xbc.shape
    nc = s // SSD_CHUNK
    d = 1 if reverse else 0
    cmap = (lambda ci: nc - 1 - ci) if reverse else (lambda ci: ci)
    add_skip = skip is not None
    ins = [xbc, xbc, xbc, dt, bias, alog, s0] + (list(skip) if add_skip else [])
    in_specs = [
        pl.BlockSpec((1, SSD_CHUNK, SSD_DI), lambda bi, ci: (bi, cmap(ci), 0)),
        pl.BlockSpec((1, SSD_CHUNK, 512), lambda bi, ci: (bi, cmap(ci), 4)),
        pl.BlockSpec((1, SSD_CHUNK, 512), lambda bi, ci: (bi, cmap(ci), 5)),
        pl.BlockSpec((1, SSD_CHUNK, LANE), lambda bi, ci: (bi, cmap(ci), d)),
        pl.BlockSpec((1, LANE), lambda bi, ci: (0, d)),
        pl.BlockSpec((1, LANE), lambda bi, ci: (0, d)),
        pl.BlockSpec((1, SSD_DI, SSD_N), lambda bi, ci: (bi, 0, 0)),
    ]
    if add_skip:
        in_specs += [pl.BlockSpec((1, SSD_DI), lambda bi, ci: (0, 0)),
                     pl.BlockSpec((1, SSD_CHUNK, SSD_DI), lambda bi, ci: (bi, cmap(ci), 0))]
    return pl.pallas_call(
        functools.partial(_ssd_scan_kernel, reverse=reverse, add_skip=add_skip),
        out_shape=[jax.ShapeDtypeStruct((bsz, s, SSD_DI), F32), jax.ShapeDtypeStruct((bsz, SSD_DI, SSD_N), F32)],
        grid=(bsz, nc), in_specs=in_specs,
        out_specs=[pl.BlockSpec((1, SSD_CHUNK, SSD_DI), lambda bi, ci: (bi, cmap(ci), 0)),
                   pl.BlockSpec((1, SSD_DI, SSD_N), lambda bi, ci: (bi, 0, 0))],
        scratch_shapes=[pltpu.VMEM((SSD_DI, SSD_N), F32)],
        compiler_params=_cparams("parallel", "arbitrary"), name="ssd_scan_bwd" if reverse else "ssd_scan_fwd",
    )(*ins)


def _ssd_out_kernel(y_ref, z_ref, x_ref, mod_ref, ng_ref, w_ref, o_ref):
    y = y_ref[...] * _silu(z_ref[...].astype(F32))
    gw = SSD_DI // SSD_G
    parts = []
    for g in range(SSD_G):
        yg = y[:, g * gw:(g + 1) * gw]
        parts.append((yg * _rms(yg, gw) * ng_ref[:, g * gw:(g + 1) * gw]).astype(BF16))
    yn = jnp.concatenate(parts, axis=1)
    o_ref[...] = x_ref[...] + mod_ref[0, 2:3, :] * _dot(yn, w_ref[...])


def _ssd_out(y, z, x, mod, ng, w, *, seq, per_batch_mod, tm):
    t = x.shape[0]
    steps_per_seq = seq // tm
    mod_map = (lambda i: (i // steps_per_seq, 0, 0)) if per_batch_mod else (lambda i: (0, 0, 0))
    row = lambda n: pl.BlockSpec((tm, n), lambda i: (i, 0))
    return pl.pallas_call(
        _ssd_out_kernel,
        out_shape=jax.ShapeDtypeStruct((t, D), F32), grid=(t // tm,),
        in_specs=[row(SSD_DI), row(SSD_DI), row(D), pl.BlockSpec((1, 6, D), mod_map),
                  pl.BlockSpec(ng.shape, lambda i: (0, 0)), pl.BlockSpec(w.shape, lambda i: (0, 0))],
        out_specs=row(D),
        compiler_params=_cparams("parallel"), name="ssd_out",
    )(y, z, x, mod, ng, w)


def _pad_chunks(w, n_chunks, width, to):
    lead = w.shape[:-1]
    w = w.reshape(lead + (n_chunks, width))
    w = jnp.pad(w, [(0, 0)] * len(lead) + [(0, 0), (0, to - width)])
    return w.reshape(lead + (n_chunks * to,))


def _attn_weights(w_in, a_qk_g, b_qa_g, b_wq_up, b_kva_g, b_qk_g):
    c = [0, 512, 1024, 1536, 1920, 2176, 2240]
    w_in_p = jnp.concatenate([
        _pad_chunks(w_in[:, c[0]:c[1]], 8, A_DQK, LANE), _pad_chunks(w_in[:, c[1]:c[2]], 8, A_DQK, LANE),
        w_in[:, c[2]:c[5]], _pad_chunks(w_in[:, c[5]:c[6]], 1, B_ROPE, LANE)], axis=1).astype(BF16)
    gq = jnp.tile(jnp.pad(a_qk_g[0] * (A_DQK ** -0.5), (0, LANE - A_DQK)), 8)[None]
    gk = jnp.tile(jnp.pad(a_qk_g[1], (0, LANE - A_DQK)), 8)[None]
    wq = _pad_chunks(b_wq_up, B_HEADS, B_DQK, 256).astype(BF16)
    gbq = jnp.tile(jnp.pad(b_qk_g[0] * (B_DQK ** -0.5), (0, 256 - B_DQK)), B_HEADS)[None]
    gbk = jnp.pad(b_qk_g[1], (0, 256 - B_DQK))[None]
    return dict(w_in=w_in_p, gq=gq, gk=gk, gqa=b_qa_g[None], wq=wq, gbq=gbq, gkva=b_kva_g[None]), gbk


def _tile(seq, cap):
    return min(seq, cap)


def _attn_layer(xp, xs, shapes, modp, mods, g1, wp, gbk, wkv, lam_p, gsub, w_out, ctx, lam_init):
    (bp, sp), (bs, ss) = shapes
    ctx_ka, ctx_va, ctx_ckv, ctx_kr = ctx
    past = ctx_ka.shape[1]
    tabs_s = _rope_tables(ss)
    tabs_p = _rope_tables(sp)
    tmp, tms = _tile(sp, 256), _tile(ss, 512)
    qa, ka, va, qb, ckv, kr = _attn_in(xp, modp, g1, wp, tabs_p, seq=sp, per_batch_mod=False, rope=False,
                                      kv_dtype=F32, tm=tmp)
    kb, vb = _mla_expand(ckv, kr, wkv, gbk, tabs_p, seq=sp, rope=False, tm=tmp)
    r3 = lambda a, b_, s_: a.reshape(b_, s_, a.shape[-1])
    o_p = _attention(r3(qa, bp, sp), r3(qb, bp, sp), lam_p, gsub,
                     [(r3(ka, bp, sp), r3(va, bp, sp), r3(kb, bp, sp), r3(vb, bp, sp))],
                     lam_init=lam_init, tq=_tile(sp, 256))
    new_ak = ka.reshape(bp, sp, 2 * A_HEADS, LANE)[..., :A_DQK].reshape(bp, 1, sp, A_HEADS, 2, A_DQK)
    new_av = va.reshape(bp, 1, sp, A_HEADS, A_DV)
    new_ckv = ckv.reshape(bp, 1, sp, B_KV_LORA)
    new_kr = kr[:, :B_ROPE].reshape(bp, 1, sp, B_ROPE)
    xp1 = _proj_res(o_p.reshape(bp * sp, D), xp, modp, w_out, seq=sp, per_batch_mod=False, gate_row=2, tm=tmp)
    qa, ka, va, qb, ckv, kr = _attn_in(xs, mods, g1, wp, tabs_s, seq=ss, per_batch_mod=True, rope=True,
                                      kv_dtype=BF16, tm=tms)
    kb, vb = _mla_expand(ckv, kr, wkv, gbk, tabs_s, seq=ss, rope=True, tm=tms)
    tpast = _tile(past, 512)
    tabs_c = _rope_tables(past)
    kb_c, vb_c = _mla_expand(ctx_ckv.reshape(bs * past, B_KV_LORA),
                             jnp.pad(ctx_kr.reshape(bs * past, B_ROPE), ((0, 0), (0, LANE - B_ROPE))),
                             wkv, gbk, tabs_c, seq=past, rope=False, tm=tpast)
    ka_c = _pad_chunks(ctx_ka.reshape(bs, past, 2 * A_HEADS * A_DQK), 2 * A_HEADS, A_DQK, LANE).astype(BF16)
    va_c = ctx_va.reshape(bs, past, A_HEADS * A_DV).astype(BF16)
    o_s = _attention(r3(qa, bs, ss), r3(qb, bs, ss), lam_p, gsub,
                     [(ka_c, va_c, r3(kb_c, bs, past), r3(vb_c, bs, past)),
                      (r3(ka, bs, ss), r3(va, bs, ss), r3(kb, bs, ss), r3(vb, bs, ss))],
                     lam_init=lam_init, tq=_tile(ss, 256))
    xs1 = _proj_res(o_s.reshape(bs * ss, D), xs, mods, w_out, seq=ss, per_batch_mod=True, gate_row=2, tm=tms)
    return xp1, xs1, (new_ak, new_av, new_ckv, new_kr)


def _ssd_stream(x, mod, g1, w_in, conv_w, conv_b, bias, alog, dvec, ng, w_out, s0f, s0b, *, bsz, seq, per_batch_mod, tm):
    z, xbc, dt = _ssd_in(x, mod, g1, w_in, seq=seq, per_batch_mod=per_batch_mod, tm=tm)
    xbc = _conv_silu(xbc.reshape(bsz, seq, SSD_CONV_DIM), conv_w, conv_b, cw=512)
    dt = dt.reshape(bsz, seq, 2 * LANE)
    yf, sf = _ssd_scan(xbc, dt, bias, alog, s0f, reverse=False)
    y, sb = _ssd_scan(xbc, dt, bias, alog, s0b, (dvec, yf), reverse=True)
    x1 = _ssd_out(y.reshape(bsz * seq, SSD_DI), z, x, mod, ng, w_out, seq=seq, per_batch_mod=per_batch_mod, tm=tm)
    return x1, sf, sb


def kernel(x_prompt, x_sample, cache_a_k, cache_a_v, cache_b_ckv, cache_b_krope, state_ssd, c, c_ctx, mod_w, mod_b, norm1_g, norm2_g, attn_w_in, a_qk_g, a_lambda, a_sub_g, b_qa_g, b_wq_up, b_kva_g, b_wkv_up, b_qk_g, attn_w_out, ssd_w_in, ssd_conv_w, ssd_conv_b, ssd_dt_bias, ssd_a_log, ssd_d, ssd_norm_g, ssd_w_out, router_w, router_bias, moe_w_in, moe_w_out, shared_w_in, shared_w_out):
    bp, sp, _ = x_prompt.shape
    bs, ss, _ = x_sample.shape
    depth = mod_w.shape[0]
    shapes = ((bp, sp), (bs, ss))
    xp = x_prompt.reshape(bp * sp, D)
    xs = x_sample.reshape(bs * ss, D)

    n_c = 1 + bs
    rows = -(-n_c // 16) * 16
    c_all = jnp.pad(jnp.concatenate([c_ctx[None], c], axis=0), ((0, rows - n_c), (0, 0)))
    mod = _mod_vectors(c_all, mod_w, mod_b).reshape(depth, rows, 6, D)

    rw_t = jnp.pad(router_w.T, ((0, LANE - N_EXPERTS), (0, 0)))
    rb_col = jnp.pad(router_bias, (0, LANE - N_EXPERTS))[:, None]
    tmp, tms = _tile(sp, 256), _tile(ss, 512)

    new_caches = []
    new_ssd = []
    for l in range(depth):
        i = l // 2
        modp = mod[l, 0:1]
        mods = mod[l, 1:1 + bs]
        g1 = norm1_g[l][None]
        g2 = norm2_g[l][None]
        if l % 2 == 0:
            lam_init = 0.8 - 0.6 * math.exp(-0.3 * l)
            wp, gbk = _attn_weights(attn_w_in[i], a_qk_g[i], b_qa_g[i], b_wq_up[i], b_kva_g[i], b_qk_g[i])
            ctx = (cache_a_k[:, i], cache_a_v[:, i], cache_b_ckv[:, i], cache_b_krope[:, i])
            xp, xs, caches = _attn_layer(xp, xs, shapes, modp, mods, g1, wp, gbk, b_wkv_up[i].astype(BF16),
                                         a_lambda[i], a_sub_g[i][None], attn_w_out[i].astype(BF16), ctx, lam_init)
            new_caches.append(caches)
        else:
            w_in = ssd_w_in[i]
            n0 = SSD_DI + SSD_CONV_DIM
            w_in_p = jnp.concatenate([w_in[:, :n0], _pad_chunks(w_in[:, n0:], 2, SSD_H, LANE)], axis=1).astype(BF16)
            bias = _pad_chunks(ssd_dt_bias[i].reshape(1, 2 * SSD_H), 2, SSD_H, LANE)
            alog = _pad_chunks(ssd_a_log[i].reshape(1, 2 * SSD_H), 2, SSD_H, LANE)
            dvec = jnp.repeat(ssd_d[i], SSD_P)[None]
            args = (g1, w_in_p, ssd_conv_w[i], ssd_conv_b[i][None], bias, alog, dvec, ssd_norm_g[i][None],
                    ssd_w_out[i].astype(BF16))
            zero = jnp.zeros((bp, SSD_DI, SSD_N), F32)
            xp, sf, sb = _ssd_stream(xp, modp, *args, zero, zero, bsz=bp, seq=sp, per_batch_mod=False, tm=tmp)
            xs, _, _ = _ssd_stream(xs, mods, *args, state_ssd[:, i, 0].reshape(bs, SSD_DI, SSD_N),
                                   state_ssd[:, i, 1].reshape(bs, SSD_DI, SSD_N), bsz=bs, seq=ss,
                                   per_batch_mod=True, tm=tms)
            new_ssd.append(jnp.stack([sf, sb], axis=1).reshape(bp, 2, SSD_H, SSD_P, SSD_N))
        win_all = jnp.concatenate([moe_w_in[l], shared_w_in[l][None]], axis=0).astype(BF16)
        wout_all = jnp.concatenate([moe_w_out[l], shared_w_out[l][None]], axis=0).astype(BF16)
        xp = _moe_dense(xp, modp, g2, rw_t, rb_col, win_all, wout_all, seq=sp, per_batch_mod=False, tm=tmp)
        xs = _moe_dense(xs, mods, g2, rw_t, rb_col, win_all, wout_all, seq=ss, per_batch_mod=True, tm=tms)

    cat = lambda k: jnp.concatenate([cc[k] for cc in new_caches], axis=1)
    return (xp.reshape(bp, sp, D), xs.reshape(bs, ss, D), cat(0), cat(1), cat(2), cat(3),
            jnp.stack(new_ssd, axis=1))
```

```python
import functools
import math

import jax
import jax.numpy as jnp
from jax import lax
from jax.experimental import pallas as pl
from jax.experimental.pallas import tpu as pltpu

F32 = jnp.float32
BF16 = jnp.bfloat16

D = 1024
EPS = 1e-6
GRID_W = 64
ROPE_BASE = 10000.0
LANE = 128

A_HEADS = 4
A_DQK = 64
A_DV = 128
B_HEADS = 4
B_NOPE = 128
B_ROPE = 64
B_DQK = B_NOPE + B_ROPE
B_DV = 128
B_Q_LORA = 384
B_KV_LORA = 256

SSD_DI = 2048
SSD_P = 64
SSD_H = 32
SSD_N = 128
SSD_G = 4
SSD_CHUNK = 128
SSD_CONV_DIM = SSD_DI + 2 * SSD_G * SSD_N

N_EXPERTS = 16
N_GROUPS = 4
EPG = 4
D_EXPERT = 256

VMEM_LIMIT = 56 * 1024 * 1024


def _cparams(*sem):
    return pltpu.CompilerParams(dimension_semantics=sem, vmem_limit_bytes=VMEM_LIMIT)


def _dot(a, b):
    return jnp.dot(a, b, preferred_element_type=F32)


def _dot_nt(a, b):
    return lax.dot_general(a, b, (((1,), (1,)), ((), ())), preferred_element_type=F32)


def _split3(x):
    x1 = x.astype(BF16)
    r = x - x1.astype(F32)
    x2 = r.astype(BF16)
    x3 = (r - x2.astype(F32)).astype(BF16)
    return x1, x2, x3


def _silu(x):
    return x * (1.0 / (1.0 + jnp.exp(-x)))


def _rms(x, n):
    return lax.rsqrt(jnp.sum(x * x, axis=-1, keepdims=True) * (1.0 / n) + EPS)


def _modulated(x, g, shift, scale):
    return x * _rms(x, D) * g * (1.0 + scale) + shift


def _mod_kernel(c_ref, w_ref, b_ref, o_ref):
    c = _silu(c_ref[...])
    w = w_ref[0]
    c1, c2, c3 = _split3(c)
    w1, w2, w3 = _split3(w)
    acc = _dot(c1, w1) + (_dot(c1, w2) + _dot(c2, w1)) + (_dot(c2, w2) + _dot(c1, w3) + _dot(c3, w1))
    o_ref[0] = acc + b_ref[0]


def _mod_vectors(c_all, mod_w, mod_b):
    depth = mod_w.shape[0]
    rows = c_all.shape[0]
    tn = 1536
    return pl.pallas_call(
        _mod_kernel,
        out_shape=jax.ShapeDtypeStruct((depth, rows, 6 * D), F32),
        grid=(depth, 6 * D // tn),
        in_specs=[
            pl.BlockSpec((rows, D), lambda l, j: (0, 0)),
            pl.BlockSpec((1, D, tn), lambda l, j: (l, 0, j)),
            pl.BlockSpec((1, 1, tn), lambda l, j: (l, 0, j)),
        ],
        out_specs=pl.BlockSpec((1, rows, tn), lambda l, j: (l, 0, j)),
        compiler_params=_cparams("parallel", "parallel"),
        name="mod_vectors",
    )(c_all, mod_w, mod_b.reshape(depth, 1, 6 * D))


def _rope_tables(n_tok):
    t = jnp.arange(n_tok)
    row = (t // GRID_W).astype(F32)
    col = (t % GRID_W).astype(F32)
    half = B_ROPE // 2
    inv = ROPE_BASE ** (-2.0 * jnp.arange(half // 2, dtype=F32) / half)
    ang = jnp.concatenate([row[:, None] * inv, col[:, None] * inv], axis=-1)
    cos, sin = jnp.cos(ang), jnp.sin(ang)
    zero = jnp.zeros((n_tok, LANE - 2 * half), F32)
    cos_t = jnp.concatenate([cos, cos, zero], axis=-1)
    sin_t = jnp.concatenate([-sin, sin, zero], axis=-1)
    return cos_t, sin_t


def _rope_chunk(c, cos, sin):
    lane = lax.broadcasted_iota(jnp.int32, c.shape, 1)
    swapped = jnp.where(lane < 32, pltpu.roll(c, LANE - 32, 1), pltpu.roll(c, 32, 1))
    return c * cos + swapped * sin


N_ATTN_COLS = 3328
_AQ0, _AK0, _AV0, _BQ0, _BKV0, _BKR0 = 0, 1024, 2048, 2560, 2944, 3200


def _attn_in_kernel(x_ref, mod_ref, g1_ref, w_ref, gq_ref, gk_ref, gqa_ref, wq_ref, gbq_ref, gkva_ref,
                    cos_ref, sin_ref, qa_ref, ka_ref, va_ref, qb_ref, ckv_ref, kr_ref, *, rope):
    x = x_ref[...]
    h = _modulated(x, g1_ref[...], mod_ref[0, 0:1, :], mod_ref[0, 1:2, :])
    z = _dot(h.astype(BF16), w_ref[...])
    if rope:
        cos = cos_ref[...]
        sin = sin_ref[...]
    for j in range(2 * A_HEADS):
        sl = slice(j * LANE, (j + 1) * LANE)
        q = z[:, _AQ0 + j * LANE:_AQ0 + (j + 1) * LANE]
        q = q * _rms(q, A_DQK) * gq_ref[:, sl]
        k = z[:, _AK0 + j * LANE:_AK0 + (j + 1) * LANE]
        k = k * _rms(k, A_DQK) * gk_ref[:, sl]
        if rope:
            q = _rope_chunk(q, cos, sin)
            k = _rope_chunk(k, cos, sin)
        qa_ref[:, sl] = q.astype(qa_ref.dtype)
        ka_ref[:, sl] = k.astype(ka_ref.dtype)
    va_ref[...] = z[:, _AV0:_AV0 + A_HEADS * A_DV].astype(va_ref.dtype)
    qc = z[:, _BQ0:_BQ0 + B_Q_LORA]
    qc = qc * _rms(qc, B_Q_LORA) * gqa_ref[...]
    qb = _dot(qc.astype(BF16), wq_ref[...])
    for hh in range(B_HEADS):
        lo = qb[:, hh * 256:hh * 256 + LANE]
        hi = qb[:, hh * 256 + LANE:(hh + 1) * 256]
        inv = lax.rsqrt((jnp.sum(lo * lo, axis=-1, keepdims=True)
                         + jnp.sum(hi * hi, axis=-1, keepdims=True)) * (1.0 / B_DQK) + EPS)
        lo = lo * inv * gbq_ref[:, hh * 256:hh * 256 + LANE]
        hi = hi * inv * gbq_ref[:, hh * 256 + LANE:(hh + 1) * 256]
        if rope:
            hi = _rope_chunk(hi, cos, sin)
        qb_ref[:, hh * 256:hh * 256 + LANE] = lo.astype(qb_ref.dtype)
        qb_ref[:, hh * 256 + LANE:(hh + 1) * 256] = hi.astype(qb_ref.dtype)
    kvc = z[:, _BKV0:_BKV0 + B_KV_LORA]
    ckv_ref[...] = kvc * _rms(kvc, B_KV_LORA) * gkva_ref[...]
    kr_ref[...] = z[:, _BKR0:_BKR0 + LANE]


def _attn_in(x, mod, g1, wp, rope_tabs, *, seq, per_batch_mod, rope, kv_dtype, tm):
    t = x.shape[0]
    steps_per_seq = seq // tm
    mod_map = (lambda i: (i // steps_per_seq, 0, 0)) if per_batch_mod else (lambda i: (0, 0, 0))
    tab_map = lambda i: (i % steps_per_seq, 0)
    row = lambda n: pl.BlockSpec((tm, n), lambda i: (i, 0))
    full = lambda a: pl.BlockSpec(a.shape, lambda i: (0,) * a.ndim)
    cos_t, sin_t = rope_tabs
    ins = [x, mod, g1, wp["w_in"], wp["gq"], wp["gk"], wp["gqa"], wp["wq"], wp["gbq"], wp["gkva"], cos_t, sin_t]
    in_specs = [row(D), pl.BlockSpec((1, 6, D), mod_map), full(g1), full(wp["w_in"]), full(wp["gq"]),
                full(wp["gk"]), full(wp["gqa"]), full(wp["wq"]), full(wp["gbq"]), full(wp["gkva"]),
                pl.BlockSpec((tm, LANE), tab_map), pl.BlockSpec((tm, LANE), tab_map)]
    out_shape = [
        jax.ShapeDtypeStruct((t, 1024), BF16),
        jax.ShapeDtypeStruct((t, 1024), kv_dtype),
        jax.ShapeDtypeStruct((t, 512), kv_dtype),
        jax.ShapeDtypeStruct((t, 1024), BF16),
        jax.ShapeDtypeStruct((t, B_KV_LORA), F32),
        jax.ShapeDtypeStruct((t, LANE), F32),
    ]
    out_specs = [row(1024), row(1024), row(512), row(1024), row(B_KV_LORA), row(LANE)]
    return pl.pallas_call(
        functools.partial(_attn_in_kernel, rope=rope),
        out_shape=out_shape, grid=(t // tm,), in_specs=in_specs, out_specs=out_specs,
        compiler_params=_cparams("parallel"), name="attn_in",
    )(*ins)


def _mla_expand_kernel(ckv_ref, kr_ref, w_ref, g_ref, cos_ref, sin_ref, kb_ref, vb_ref, *, rope):
    kv = _dot(ckv_ref[...].astype(BF16), w_ref[...])
    kr = kr_ref[...]
    kr_ss = jnp.sum(kr * kr, axis=-1, keepdims=True)
    for hh in range(B_HEADS):
        kn = kv[:, hh * 256:hh * 256 + LANE]
        inv = lax.rsqrt((jnp.sum(kn * kn, axis=-1, keepdims=True) + kr_ss) * (1.0 / B_DQK) + EPS)
        hi = kr * inv * g_ref[:, LANE:]
        if rope:
            hi = _rope_chunk(hi, cos_ref[...], sin_ref[...])
        kb_ref[:, hh * 256:hh * 256 + LANE] = (kn * inv * g_ref[:, :LANE]).astype(kb_ref.dtype)
        kb_ref[:, hh * 256 + LANE:(hh + 1) * 256] = hi.astype(kb_ref.dtype)
        vb_ref[:, hh * LANE:(hh + 1) * LANE] = kv[:, hh * 256 + LANE:(hh + 1) * 256].astype(vb_ref.dtype)


def _mla_expand(ckv, kr, wkv, gk, rope_tabs, *, seq, rope, tm):
    t = ckv.shape[0]
    steps_per_seq = seq // tm
    tab_map = lambda i: (i % steps_per_seq, 0)
    row = lambda n: pl.BlockSpec((tm, n), lambda i: (i, 0))
    full = lambda a: pl.BlockSpec(a.shape, lambda i: (0,) * a.ndim)
    cos_t, sin_t = rope_tabs
    return pl.pallas_call(
        functools.partial(_mla_expand_kernel, rope=rope),
        out_shape=[jax.ShapeDtypeStruct((t, 1024), BF16), jax.ShapeDtypeStruct((t, 512), BF16)],
        grid=(t // tm,),
        in_specs=[row(B_KV_LORA), row(LANE), full(wkv), full(gk),
                  pl.BlockSpec((tm, LANE), tab_map), pl.BlockSpec((tm, LANE), tab_map)],
        out_specs=[row(1024), row(512)],
        compiler_params=_cparams("parallel"), name="mla_expand",
    )(ckv, kr, wkv, gk, cos_t, sin_t)


def _softmax_pieces(q, keys):
    s = [_dot_nt(q, k) for k in keys]
    mx = functools.reduce(jnp.maximum, [jnp.max(si, axis=-1, keepdims=True) for si in s])
    p = [jnp.exp2(si - mx) for si in s]
    den = functools.reduce(jnp.add, [jnp.sum(pi, axis=-1, keepdims=True) for pi in p])
    return p, 1.0 / den


def _attn_kernel(*refs, n_pieces, lam_init):
    qa_ref, qb_ref, lam_ref, gsub_ref = refs[:4]
    kv_refs = refs[4:4 + 4 * n_pieces]
    o_ref = refs[4 + 4 * n_pieces]
    ka = kv_refs[0::4]
    va = kv_refs[1::4]
    kb = kv_refs[2::4]
    vb = kv_refs[3::4]
    lam_p = lam_ref[...]
    lam = (jnp.exp(jnp.sum(lam_p[0:1] * lam_p[1:2], axis=-1, keepdims=True))
           - jnp.exp(jnp.sum(lam_p[2:3] * lam_p[3:4], axis=-1, keepdims=True)) + lam_init)
    for h in range(A_HEADS):
        ps = []
        for m in range(2):
            sl = slice((2 * h + m) * LANE, (2 * h + m + 1) * LANE)
            ps.append(_softmax_pieces(qa_ref[0, :, sl], [k[0, :, sl].astype(BF16) for k in ka]))
        (p0, r0), (p1, r1) = ps
        ratio = lam * r1 / r0
        o = None
        for i in range(n_pieces):
            w = (p0[i] - ratio * p1[i]).astype(BF16)
            oi = _dot(w, va[i][0, :, h * A_DV:(h + 1) * A_DV].astype(BF16))
            o = oi if o is None else o + oi
        o = o * r0
        o = o * _rms(o, A_DV) * gsub_ref[...] * (1.0 - lam_init)
        o_ref[0, :, h * A_DV:(h + 1) * A_DV] = o.astype(o_ref.dtype)
    for h in range(B_HEADS):
        sl = slice(h * 256, (h + 1) * 256)
        p, rden = _softmax_pieces(qb_ref[0, :, sl], [k[0, :, sl] for k in kb])
        o = None
        for i in range(n_pieces):
            oi = _dot(p[i].astype(BF16), vb[i][0, :, h * B_DV:(h + 1) * B_DV])
            o = oi if o is None else o + oi
        o = o * rden
        c0 = A_HEADS * A_DV + h * B_DV
        o_ref[0, :, c0:c0 + B_DV] = o.astype(o_ref.dtype)


def _attention(qa, qb, lam_p, gsub, pieces, *, lam_init, tq):
    b, s, _ = qa.shape
    qspec = pl.BlockSpec((1, tq, 1024), lambda bi, qi: (bi, qi, 0))
    full = lambda a: pl.BlockSpec(a.shape, lambda bi, qi: (0,) * a.ndim)
    ins = [qa, qb, lam_p, gsub]
    in_specs = [qspec, qspec, full(lam_p), full(gsub)]
    for piece in pieces:
        for a in piece:
            ins.append(a)
            in_specs.append(pl.BlockSpec((1,) + a.shape[1:], lambda bi, qi: (bi, 0, 0)))
    return pl.pallas_call(
        functools.partial(_attn_kernel, n_pieces=len(pieces), lam_init=lam_init),
        out_shape=jax.ShapeDtypeStruct((b, s, 1024), BF16),
        grid=(b, s // tq), in_specs=in_specs,
        out_specs=pl.BlockSpec((1, tq, 1024), lambda bi, qi: (bi, qi, 0)),
        compiler_params=_cparams("parallel", "parallel"), name="attn_core",
    )(*ins)


def _proj_res_kernel(o_ref, x_ref, mod_ref, w_ref, y_ref, *, gate_row):
    y_ref[...] = x_ref[...] + mod_ref[0, gate_row:gate_row + 1, :] * _dot(o_ref[...], w_ref[...])


def _proj_res(o, x, mod, w, *, seq, per_batch_mod, gate_row, tm):
    t, k = o.shape
    steps_per_seq = seq // tm
    mod_map = (lambda i: (i // steps_per_seq, 0, 0)) if per_batch_mod else (lambda i: (0, 0, 0))
    return pl.pallas_call(
        functools.partial(_proj_res_kernel, gate_row=gate_row),
        out_shape=jax.ShapeDtypeStruct((t, D), F32), grid=(t // tm,),
        in_specs=[pl.BlockSpec((tm, k), lambda i: (i, 0)), pl.BlockSpec((tm, D), lambda i: (i, 0)),
                  pl.BlockSpec((1, 6, D), mod_map), pl.BlockSpec(w.shape, lambda i: (0, 0))],
        out_specs=pl.BlockSpec((tm, D), lambda i: (i, 0)),
        compiler_params=_cparams("parallel"), name="proj_res",
    )(o, x, mod, w)


def _route(logits_t, bias_col):
    tm = logits_t.shape[1]
    scores = 1.0 / (1.0 + jnp.exp(-logits_t[:N_EXPERTS]))
    sel = scores + bias_col[:N_EXPERTS]
    grp = []
    for g in range(N_GROUPS):
        r = [sel[EPG * g + j:EPG * g + j + 1] for j in range(EPG)]
        best2 = None
        for a in range(EPG):
            for b2 in range(a + 1, EPG):
                pair = r[a] + r[b2]
                best2 = pair if best2 is None else jnp.maximum(best2, pair)
        grp.append(best2)
    best = jnp.zeros((1, tm), jnp.int32)
    best_v = grp[0]
    for g in range(1, N_GROUPS):
        better = grp[g] > best_v
        best = jnp.where(better, g, best)
        best_v = jnp.where(better, grp[g], best_v)
    v = []
    sc = []
    for j in range(EPG):
        vj = sel[j:j + 1]
        sj = scores[j:j + 1]
        for g in range(1, N_GROUPS):
            vj = jnp.where(best == g, sel[EPG * g + j:EPG * g + j + 1], vj)
            sj = jnp.where(best == g, scores[EPG * g + j:EPG * g + j + 1], sj)
        v.append(vj)
        sc.append(sj)
    picked = []
    for j in range(EPG):
        rank = jnp.zeros((1, tm), jnp.int32)
        for k in range(EPG):
            if k == j:
                continue
            ahead = (v[k] >= v[j]) if k < j else (v[k] > v[j])
            rank = rank + ahead.astype(jnp.int32)
        picked.append(rank < 2)
    den = functools.reduce(jnp.add, [jnp.where(picked[j], sc[j], 0.0) for j in range(EPG)])
    gates = [jnp.where(picked[j], sc[j] / den, 0.0) for j in range(EPG)]
    return best, gates


def _moe_route_kernel(x_ref, mod_ref, g2_ref, rw_ref, rb_ref, h_ref, info_ref, carry):
    tm = x_ref.shape[0]

    @pl.when(pl.program_id(0) == 0)
    def _():
        carry[...] = jnp.zeros_like(carry)

    h = _modulated(x_ref[...], g2_ref[...], mod_ref[0, 3:4, :], mod_ref[0, 4:5, :])
    h_ref[...] = h
    h1, h2, h3 = _split3(h)
    w1, w2, w3 = _split3(rw_ref[...])
    logits_t = (_dot_nt(w1, h1) + (_dot_nt(w1, h2) + _dot_nt(w2, h1))
                + (_dot_nt(w2, h2) + _dot_nt(w1, h3) + _dot_nt(w3, h1)))
    best, gates = _route(logits_t, rb_ref[...])
    grow = lax.broadcasted_iota(jnp.int32, (8, tm), 0)
    onehot = jnp.where(grow == best, 1.0, 0.0)
    ti = lax.broadcasted_iota(jnp.int32, (tm, tm), 0)
    tj = lax.broadcasted_iota(jnp.int32, (tm, tm), 1)
    upper = jnp.where(ti <= tj, 1.0, 0.0).astype(BF16)
    cum = _dot(onehot.astype(BF16), upper) + carry[:, 0:1]
    rank = jnp.sum(onehot * cum, axis=0, keepdims=True) - 1.0
    carry[...] = jnp.broadcast_to(cum[:, tm - 1:tm], carry.shape)
    info_ref[...] = jnp.concatenate(gates + [best.astype(F32), rank, jnp.zeros((2, tm), F32)], axis=0)


def _moe_route(x, mod, g2, rw_t, rb_col, *, seq, per_batch_mod, tm):
    t = x.shape[0]
    steps_per_seq = seq // tm
    mod_map = (lambda i: (i // steps_per_seq, 0, 0)) if per_batch_mod else (lambda i: (0, 0, 0))
    full = lambda a: pl.BlockSpec(a.shape, lambda i: (0,) * a.ndim)
    return pl.pallas_call(
        _moe_route_kernel,
        out_shape=[jax.ShapeDtypeStruct((t, D), F32), jax.ShapeDtypeStruct((8, t), F32)],
        grid=(t // tm,),
        in_specs=[pl.BlockSpec((tm, D), lambda i: (i, 0)), pl.BlockSpec((1, 6, D), mod_map), full(g2),
                  full(rw_t), full(rb_col)],
        out_specs=[pl.BlockSpec((tm, D), lambda i: (i, 0)), pl.BlockSpec((8, tm), lambda i: (0, i))],
        scratch_shapes=[pltpu.VMEM((8, LANE), F32)],
        compiler_params=_cparams("arbitrary"), name="moe_route",
    )(x, mod, g2, rw_t, rb_col)


def _dispatch_plan(info, t, tm):
    grp = info[4].astype(jnp.int32)
    rank = info[5].astype(jnp.int32)
    n_slots = t + N_GROUPS * tm
    n_tiles = n_slots // tm
    counts = jnp.sum((grp[None, :] == jnp.arange(N_GROUPS)[:, None]).astype(jnp.int32), axis=1)
    tiles = (counts + tm - 1) // tm
    start_tile = jnp.cumsum(tiles) - tiles
    dest = start_tile[grp] * tm + rank
    src = jnp.zeros((n_slots,), jnp.int32).at[dest].set(jnp.arange(t, dtype=jnp.int32), unique_indices=True)
    valid = jnp.zeros((n_slots,), jnp.bool_).at[dest].set(True, unique_indices=True)
    slot = jnp.arange(n_slots, dtype=jnp.int32)
    dump = t + ((slot // tm) % 2) * tm + slot % tm
    dst = jnp.where(valid, src, dump)
    tile_group = jnp.clip(jnp.searchsorted(start_tile, jnp.arange(n_tiles), side="right") - 1, 0, N_GROUPS - 1)
    gates = jnp.where(valid[:, None], info[:EPG].T[src], 0.0)
    return (tile_group.astype(jnp.int32), src.reshape(n_tiles, 1, tm), dst.reshape(n_tiles, 1, tm), gates)


def _moe_sorted_kernel(tg_ref, src_ref, srcn_ref, dst_ref, gate_ref, win_ref, wout_ref, h_hbm, y_hbm,
                       hbuf, ybuf, gsem, ssem, *, tm, n_tiles):
    i = pl.program_id(0)
    slot = lax.rem(i, 2)

    def gather(idx_ref, s):
        for r in range(tm):
            pltpu.make_async_copy(h_hbm.at[pl.ds(idx_ref[0, 0, r], 1)], hbuf.at[s, pl.ds(r, 1)],
                                  gsem.at[s]).start()

    def gather_wait(s):
        pltpu.make_async_copy(h_hbm.at[pl.ds(0, tm)], hbuf.at[s], gsem.at[s]).wait()

    def scatter_wait(s):
        pltpu.make_async_copy(ybuf.at[s], y_hbm.at[pl.ds(0, tm)], ssem.at[s]).wait()

    @pl.when(i == 0)
    def _():
        gather(src_ref, 0)

    @pl.when(i + 1 < n_tiles)
    def _():
        gather(srcn_ref, 1 - slot)

    gather_wait(slot)
    u = _dot(hbuf[slot].astype(BF16), win_ref[0])
    parts = []
    for e in range(EPG + 1):
        ue = u[:, e * 2 * D_EXPERT:(e + 1) * 2 * D_EXPERT]
        he = _silu(ue[:, :D_EXPERT]) * ue[:, D_EXPERT:]
        if e < EPG:
            he = he * gate_ref[:, e:e + 1]
        parts.append(he.astype(BF16))
    y = _dot(jnp.concatenate(parts, axis=1), wout_ref[0])

    @pl.when(i >= 2)
    def _():
        scatter_wait(slot)

    ybuf[slot] = y
    for r in range(tm):
        pltpu.make_async_copy(ybuf.at[slot, pl.ds(r, 1)], y_hbm.at[pl.ds(dst_ref[0, 0, r], 1)],
                              ssem.at[slot]).start()

    @pl.when(i == n_tiles - 1)
    def _():
        if n_tiles >= 2:
            scatter_wait(1 - slot)
        scatter_wait(slot)


def _moe_sorted(h, plan, win_g, wout_g, *, tm):
    t = h.shape[0]
    tile_group, src, dst, gates = plan
    n_tiles = src.shape[0]
    smem_blk = lambda imap: pl.BlockSpec((1, 1, tm), imap, memory_space=pltpu.SMEM)
    grid_spec = pltpu.PrefetchScalarGridSpec(
        num_scalar_prefetch=1, grid=(n_tiles,),
        in_specs=[smem_blk(lambda i, tg: (i, 0, 0)),
                  smem_blk(lambda i, tg: (jnp.minimum(i + 1, n_tiles - 1), 0, 0)),
                  smem_blk(lambda i, tg: (i, 0, 0)),
                  pl.BlockSpec((tm, EPG), lambda i, tg: (i, 0)),
                  pl.BlockSpec((1,) + win_g.shape[1:], lambda i, tg: (tg[i], 0, 0)),
                  pl.BlockSpec((1,) + wout_g.shape[1:], lambda i, tg: (tg[i], 0, 0)),
                  pl.BlockSpec(memory_space=pl.ANY)],
        out_specs=pl.BlockSpec(memory_space=pl.ANY),
        scratch_shapes=[pltpu.VMEM((2, tm, D), F32), pltpu.VMEM((2, tm, D), F32),
                        pltpu.SemaphoreType.DMA((2,)), pltpu.SemaphoreType.DMA((2,))])
    return pl.pallas_call(
        functools.partial(_moe_sorted_kernel, tm=tm, n_tiles=n_tiles),
        out_shape=jax.ShapeDtypeStruct((t + 2 * tm, D), F32), grid_spec=grid_spec,
        compiler_params=_cparams("arbitrary"), name="moe_sorted",
    )(tile_group, src, src, dst, gates, win_g, wout_g, h)


def _moe_residual_kernel(x_ref, y_ref, mod_ref, o_ref):
    o_ref[...] = x_ref[...] + mod_ref[0, 5:6, :] * y_ref[...]


def _moe_residual(x, y, mod, *, seq, per_batch_mod, tm):
    t = x.shape[0]
    steps_per_seq = seq // tm
    mod_map = (lambda i: (i // steps_per_seq, 0, 0)) if per_batch_mod else (lambda i: (0, 0, 0))
    row = pl.BlockSpec((tm, D), lambda i: (i, 0))
    return pl.pallas_call(
        _moe_residual_kernel, out_shape=jax.ShapeDtypeStruct((t, D), F32), grid=(t // tm,),
        in_specs=[row, row, pl.BlockSpec((1, 6, D), mod_map)], out_specs=row,
        compiler_params=_cparams("parallel"), name="moe_residual",
    )(x, y, mod)


MOE_TILE = 256


def _moe_block(x, mod, g2, rw_t, rb_col, win_g, wout_g, *, seq, per_batch_mod, tm):
    h, info = _moe_route(x, mod, g2, rw_t, rb_col, seq=seq, per_batch_mod=per_batch_mod, tm=tm)
    plan = _dispatch_plan(info, x.shape[0], MOE_TILE)
    y = _moe_sorted(h, plan, win_g, wout_g, tm=MOE_TILE)
    return _moe_residual(x, y, mod, seq=seq, per_batch_mod=per_batch_mod, tm=tm)


SSD_IN_COLS = SSD_DI + SSD_CONV_DIM + 2 * LANE


def _ssd_in_kernel(x_ref, mod_ref, g1_ref, w_ref, z_ref, xbc_ref, dt_ref):
    h = _modulated(x_ref[...], g1_ref[...], mod_ref[0, 0:1, :], mod_ref[0, 1:2, :])
    hb = h.astype(BF16)
    z_ref[...] = _dot(hb, w_ref[:, :SSD_DI]).astype(z_ref.dtype)
    xbc_ref[...] = _dot(hb, w_ref[:, SSD_DI:SSD_DI + SSD_CONV_DIM]).astype(xbc_ref.dtype)
    dt_ref[...] = _dot(hb, w_ref[:, SSD_DI + SSD_CONV_DIM:])


def _ssd_in(x, mod, g1, w, *, seq, per_batch_mod, tm):
    t = x.shape[0]
    steps_per_seq = seq // tm
    mod_map = (lambda i: (i // steps_per_seq, 0, 0)) if per_batch_mod else (lambda i: (0, 0, 0))
    row = lambda n: pl.BlockSpec((tm, n), lambda i: (i, 0))
    return pl.pallas_call(
        _ssd_in_kernel,
        out_shape=[jax.ShapeDtypeStruct((t, SSD_DI), BF16), jax.ShapeDtypeStruct((t, SSD_CONV_DIM), F32),
                   jax.ShapeDtypeStruct((t, 2 * LANE), F32)],
        grid=(t // tm,),
        in_specs=[row(D), pl.BlockSpec((1, 6, D), mod_map), pl.BlockSpec(g1.shape, lambda i: (0, 0)),
                  pl.BlockSpec(w.shape, lambda i: (0, 0))],
        out_specs=[row(SSD_DI), row(SSD_CONV_DIM), row(2 * LANE)],
        compiler_params=_cparams("parallel"), name="ssd_in",
    )(x, mod, g1, w)


def _conv_kernel(x_ref, w_ref, b_ref, o_ref):
    x = x_ref[0]
    s = x.shape[0]
    rowi = lax.broadcasted_iota(jnp.int32, x.shape, 0)
    prev = jnp.where(rowi == 0, 0.0, pltpu.roll(x, 1, 0))
    nxt = jnp.where(rowi == s - 1, 0.0, pltpu.roll(x, s - 1, 0))
    y = prev * w_ref[0:1, :] + x * w_ref[1:2, :] + nxt * w_ref[2:3, :] + b_ref[...]
    o_ref[0] = _silu(y).astype(o_ref.dtype)


def _conv_silu(xbc, w, b, *, cw):
    bsz, s, c = xbc.shape
    return pl.pallas_call(
        _conv_kernel,
        out_shape=jax.ShapeDtypeStruct((bsz, s, c), BF16), grid=(bsz, c // cw),
        in_specs=[pl.BlockSpec((1, s, cw), lambda bi, j: (bi, 0, j)), pl.BlockSpec((3, cw), lambda bi, j: (0, j)),
                  pl.BlockSpec((1, cw), lambda bi, j: (0, j))],
        out_specs=pl.BlockSpec((1, s, cw), lambda bi, j: (bi, 0, j)),
        compiler_params=_cparams("parallel", "parallel"), name="conv_silu",
    )(xbc, w, b)


def _head_expander():
    return (jnp.arange(SSD_DI)[None, :] // SSD_P == jnp.arange(LANE)[:, None]).astype(BF16)


def _ssd_scan_kernel(*refs, reverse, add_skip):
    if add_skip:
        (x_ref, b_ref, c_ref, dt_ref, bias_ref, alog_ref, s0_ref, ex_ref, d_ref, yin_ref,
         y_ref, sfin_ref, state_t) = refs
    else:
        x_ref, b_ref, c_ref, dt_ref, bias_ref, alog_ref, s0_ref, ex_ref, y_ref, sfin_ref, state_t = refs
    ci = pl.program_id(1)
    q = SSD_CHUNK

    @pl.when(ci == 0)
    def _():
        state_t[...] = jnp.transpose(s0_ref[0])

    dtr = dt_ref[0] + bias_ref[...]
    dt = jnp.maximum(dtr, 0.0) + jnp.log(1.0 + jnp.exp(-jnp.abs(dtr)))
    a = dt * (-jnp.exp(alog_ref[...]))
    li = lax.broadcasted_iota(jnp.int32, (q, q), 0)
    si = lax.broadcasted_iota(jnp.int32, (q, q), 1)
    seen = (si >= li) if reverse else (si <= li)
    tri = jnp.where(seen, 1.0, 0.0).astype(BF16)
    a1, a2, a3 = _split3(a)
    cs = _dot(tri, a1) + _dot(tri, a2) + _dot(tri, a3)
    cs_t = jnp.transpose(cs)
    last = 0 if reverse else q - 1
    tot = cs[last:last + 1, :]
    dt_t = jnp.transpose(dt)
    dec = jnp.broadcast_to(jnp.exp(tot), (8, LANE))
    dec_hi = dec.astype(BF16)
    dec_lo = (dec - dec_hi.astype(F32)).astype(BF16)
    fac = jnp.concatenate([(dt * jnp.exp(tot - cs)).astype(BF16), jnp.exp(cs).astype(BF16), dec_hi, dec_lo], axis=0)
    fac = _dot(fac, ex_ref[...])
    dec_x = fac[2 * q:2 * q + 1, :] + fac[2 * q + 8:2 * q + 9, :]
    low = lax.broadcasted_iota(jnp.int32, (q, LANE), 1) < SSD_P
    for g in range(SSD_G):
        bg = b_ref[0, :, g * SSD_N:(g + 1) * SSD_N]
        cg = c_ref[0, :, g * SSD_N:(g + 1) * SSD_N]
        cb = _dot_nt(cg, bg)
        s_g = state_t[:, g * 512:(g + 1) * 512]
        y_off = _dot(cg, s_g.astype(BF16))
        xw_tiles = []
        for jj in range(4):
            j = g * 4 + jj
            cols = slice(j * LANE, (j + 1) * LANE)
            xb = x_ref[0, :, cols]
            ms = []
            for h in (2 * j, 2 * j + 1):
                seg = jnp.where(seen, cs[:, h:h + 1] - cs_t[h:h + 1, :], -jnp.inf)
                ms.append((cb * jnp.exp(seg) * dt_t[h:h + 1, :]).astype(BF16))
            zero = jnp.zeros_like(xb)
            rhs = jnp.concatenate([jnp.where(low, xb, zero), jnp.where(low, zero, xb)], axis=0)
            yt = y_off[:, jj * LANE:(jj + 1) * LANE] * fac[q:2 * q, cols] + _dot(jnp.concatenate(ms, axis=1), rhs)
            xt = xb.astype(F32)
            if add_skip:
                yt = yt + yin_ref[0, :, cols] + xt * d_ref[:, cols]
            y_ref[0, :, cols] = yt.astype(y_ref.dtype)
            xw_tiles.append((xt * fac[0:q, cols]).astype(BF16))
        xw = jnp.concatenate(xw_tiles, axis=1)
        bg_t = jnp.transpose(bg.astype(F32)).astype(BF16)
        state_t[:, g * 512:(g + 1) * 512] = s_g * dec_x[:, g * 512:(g + 1) * 512] + _dot(bg_t, xw)

    @pl.when(ci == pl.num_programs(1) - 1)
    def _():
        sfin_ref[0] = jnp.transpose(state_t[...])


def _ssd_scan(xbc, dt, bias, alog, s0, skip=None, *, reverse):
    bsz, s, _ = xbc.shape
    nc = s // SSD_CHUNK
    d = 1 if reverse else 0
    cmap = (lambda ci: nc - 1 - ci) if reverse else (lambda ci: ci)
    add_skip = skip is not None
    ins = [xbc, xbc, xbc, dt, bias, alog, s0, _head_expander()] + (list(skip) if add_skip else [])
    in_specs = [
        pl.BlockSpec((1, SSD_CHUNK, SSD_DI), lambda bi, ci: (bi, cmap(ci), 0)),
        pl.BlockSpec((1, SSD_CHUNK, 512), lambda bi, ci: (bi, cmap(ci), 4)),
        pl.BlockSpec((1, SSD_CHUNK, 512), lambda bi, ci: (bi, cmap(ci), 5)),
        pl.BlockSpec((1, SSD_CHUNK, LANE), lambda bi, ci: (bi, cmap(ci), d)),
        pl.BlockSpec((1, LANE), lambda bi, ci: (0, d)),
        pl.BlockSpec((1, LANE), lambda bi, ci: (0, d)),
        pl.BlockSpec((1, SSD_DI, SSD_N), lambda bi, ci: (bi, 0, 0)),
        pl.BlockSpec((LANE, SSD_DI), lambda bi, ci: (0, 0)),
    ]
    if add_skip:
        in_specs += [pl.BlockSpec((1, SSD_DI), lambda bi, ci: (0, 0)),
                     pl.BlockSpec((1, SSD_CHUNK, SSD_DI), lambda bi, ci: (bi, cmap(ci), 0))]
    return pl.pallas_call(
        functools.partial(_ssd_scan_kernel, reverse=reverse, add_skip=add_skip),
        out_shape=[jax.ShapeDtypeStruct((bsz, s, SSD_DI), F32), jax.ShapeDtypeStruct((bsz, SSD_DI, SSD_N), F32)],
        grid=(bsz, nc), in_specs=in_specs,
        out_specs=[pl.BlockSpec((1, SSD_CHUNK, SSD_DI), lambda bi, ci: (bi, cmap(ci), 0)),
                   pl.BlockSpec((1, SSD_DI, SSD_N), lambda bi, ci: (bi, 0, 0))],
        scratch_shapes=[pltpu.VMEM((SSD_N, SSD_DI), F32)],
        compiler_params=_cparams("parallel", "arbitrary"), name="ssd_scan_bwd" if reverse else "ssd_scan_fwd",
    )(*ins)


def _ssd_out_kernel(y_ref, z_ref, x_ref, mod_ref, ng_ref, w_ref, o_ref):
    y = y_ref[...] * _silu(z_ref[...].astype(F32))
    gw = SSD_DI // SSD_G
    parts = []
    for g in range(SSD_G):
        yg = y[:, g * gw:(g + 1) * gw]
        parts.append((yg * _rms(yg, gw) * ng_ref[:, g * gw:(g + 1) * gw]).astype(BF16))
    yn = jnp.concatenate(parts, axis=1)
    o_ref[...] = x_ref[...] + mod_ref[0, 2:3, :] * _dot(yn, w_ref[...])


def _ssd_out(y, z, x, mod, ng, w, *, seq, per_batch_mod, tm):
    t = x.shape[0]
    steps_per_seq = seq // tm
    mod_map = (lambda i: (i // steps_per_seq, 0, 0)) if per_batch_mod else (lambda i: (0, 0, 0))
    row = lambda n: pl.BlockSpec((tm, n), lambda i: (i, 0))
    return pl.pallas_call(
        _ssd_out_kernel,
        out_shape=jax.ShapeDtypeStruct((t, D), F32), grid=(t // tm,),
        in_specs=[row(SSD_DI), row(SSD_DI), row(D), pl.BlockSpec((1, 6, D), mod_map),
                  pl.BlockSpec(ng.shape, lambda i: (0, 0)), pl.BlockSpec(w.shape, lambda i: (0, 0))],
        out_specs=row(D),
        compiler_params=_cparams("parallel"), name="ssd_out",
    )(y, z, x, mod, ng, w)


def _pad_chunks(w, n_chunks, width, to):
    lead = w.shape[:-1]
    w = w.reshape(lead + (n_chunks, width))
    w = jnp.pad(w, [(0, 0)] * len(lead) + [(0, 0), (0, to - width)])
    return w.reshape(lead + (n_chunks * to,))


def _attn_weights(w_in, a_qk_g, b_qa_g, b_wq_up, b_kva_g, b_qk_g):
    c = [0, 512, 1024, 1536, 1920, 2176, 2240]
    w_in_p = jnp.concatenate([
        _pad_chunks(w_in[:, c[0]:c[1]], 8, A_DQK, LANE), _pad_chunks(w_in[:, c[1]:c[2]], 8, A_DQK, LANE),
        w_in[:, c[2]:c[5]], _pad_chunks(w_in[:, c[5]:c[6]], 1, B_ROPE, LANE)], axis=1).astype(BF16)
    log2e = 1.0 / math.log(2.0)
    gq = jnp.tile(jnp.pad(a_qk_g[0] * (A_DQK ** -0.5 * log2e), (0, LANE - A_DQK)), 8)[None]
    gk = jnp.tile(jnp.pad(a_qk_g[1], (0, LANE - A_DQK)), 8)[None]
    wq = _pad_chunks(b_wq_up, B_HEADS, B_DQK, 256).astype(BF16)
    gbq = jnp.tile(jnp.pad(b_qk_g[0] * (B_DQK ** -0.5 * log2e), (0, 256 - B_DQK)), B_HEADS)[None]
    gbk = jnp.pad(b_qk_g[1], (0, 256 - B_DQK))[None]
    return dict(w_in=w_in_p, gq=gq, gk=gk, gqa=b_qa_g[None], wq=wq, gbq=gbq, gkva=b_kva_g[None]), gbk


def _tile(seq, cap):
    return min(seq, cap)


def _attn_layer(xp, xs, shapes, modp, mods, g1, wp, gbk, wkv, lam_p, gsub, w_out, ctx, lam_init):
    (bp, sp), (bs, ss) = shapes
    ctx_ka, ctx_va, ctx_ckv, ctx_kr = ctx
    past = ctx_ka.shape[1]
    tabs_s = _rope_tables(ss)
    tabs_p = _rope_tables(sp)
    tmp, tms = _tile(sp, 256), _tile(ss, 512)
    qa, ka, va, qb, ckv, kr = _attn_in(xp, modp, g1, wp, tabs_p, seq=sp, per_batch_mod=False, rope=False,
                                      kv_dtype=F32, tm=tmp)
    kb, vb = _mla_expand(ckv, kr, wkv, gbk, tabs_p, seq=sp, rope=False, tm=tmp)
    r3 = lambda a, b_, s_: a.reshape(b_, s_, a.shape[-1])
    o_p = _attention(r3(qa, bp, sp), r3(qb, bp, sp), lam_p, gsub,
                     [(r3(ka, bp, sp), r3(va, bp, sp), r3(kb, bp, sp), r3(vb, bp, sp))],
                     lam_init=lam_init, tq=_tile(sp, 256))
    new_ak = ka.reshape(bp, sp, 2 * A_HEADS, LANE)[..., :A_DQK].reshape(bp, 1, sp, A_HEADS, 2, A_DQK)
    new_av = va.reshape(bp, 1, sp, A_HEADS, A_DV)
    new_ckv = ckv.reshape(bp, 1, sp, B_KV_LORA)
    new_kr = kr[:, :B_ROPE].reshape(bp, 1, sp, B_ROPE)
    xp1 = _proj_res(o_p.reshape(bp * sp, D), xp, modp, w_out, seq=sp, per_batch_mod=False, gate_row=2, tm=tmp)
    qa, ka, va, qb, ckv, kr = _attn_in(xs, mods, g1, wp, tabs_s, seq=ss, per_batch_mod=True, rope=True,
                                      kv_dtype=BF16, tm=tms)
    kb, vb = _mla_expand(ckv, kr, wkv, gbk, tabs_s, seq=ss, rope=True, tm=tms)
    tpast = _tile(past, 512)
    tabs_c = _rope_tables(past)
    kb_c, vb_c = _mla_expand(ctx_ckv.reshape(bs * past, B_KV_LORA),
                             jnp.pad(ctx_kr.reshape(bs * past, B_ROPE), ((0, 0), (0, LANE - B_ROPE))),
                             wkv, gbk, tabs_c, seq=past, rope=False, tm=tpast)
    ka_c = _pad_chunks(ctx_ka.reshape(bs, past, 2 * A_HEADS * A_DQK), 2 * A_HEADS, A_DQK, LANE).astype(BF16)
    va_c = ctx_va.reshape(bs, past, A_HEADS * A_DV).astype(BF16)
    o_s = _attention(r3(qa, bs, ss), r3(qb, bs, ss), lam_p, gsub,
                     [(ka_c, va_c, r3(kb_c, bs, past), r3(vb_c, bs, past)),
                      (r3(ka, bs, ss), r3(va, bs, ss), r3(kb, bs, ss), r3(vb, bs, ss))],
                     lam_init=lam_init, tq=_tile(ss, 256))
    xs1 = _proj_res(o_s.reshape(bs * ss, D), xs, mods, w_out, seq=ss, per_batch_mod=True, gate_row=2, tm=tms)
    return xp1, xs1, (new_ak, new_av, new_ckv, new_kr)


def _ssd_stream(x, mod, g1, w_in, conv_w, conv_b, bias, alog, dvec, ng, w_out, s0f, s0b, *, bsz, seq, per_batch_mod, tm):
    z, xbc, dt = _ssd_in(x, mod, g1, w_in, seq=seq, per_batch_mod=per_batch_mod, tm=tm)
    xbc = _conv_silu(xbc.reshape(bsz, seq, SSD_CONV_DIM), conv_w, conv_b, cw=512)
    dt = dt.reshape(bsz, seq, 2 * LANE)
    yf, sf = _ssd_scan(xbc, dt, bias, alog, s0f, reverse=False)
    y, sb = _ssd_scan(xbc, dt, bias, alog, s0b, (dvec, yf), reverse=True)
    x1 = _ssd_out(y.reshape(bsz * seq, SSD_DI), z, x, mod, ng, w_out, seq=seq, per_batch_mod=per_batch_mod, tm=tm)
    return x1, sf, sb


def kernel(x_prompt, x_sample, cache_a_k, cache_a_v, cache_b_ckv, cache_b_krope, state_ssd, c, c_ctx, mod_w, mod_b, norm1_g, norm2_g, attn_w_in, a_qk_g, a_lambda, a_sub_g, b_qa_g, b_wq_up, b_kva_g, b_wkv_up, b_qk_g, attn_w_out, ssd_w_in, ssd_conv_w, ssd_conv_b, ssd_dt_bias, ssd_a_log, ssd_d, ssd_norm_g, ssd_w_out, router_w, router_bias, moe_w_in, moe_w_out, shared_w_in, shared_w_out):
    bp, sp, _ = x_prompt.shape
    bs, ss, _ = x_sample.shape
    depth = mod_w.shape[0]
    shapes = ((bp, sp), (bs, ss))
    xp = x_prompt.reshape(bp * sp, D)
    xs = x_sample.reshape(bs * ss, D)

    n_c = 1 + bs
    rows = -(-n_c // 16) * 16
    c_all = jnp.pad(jnp.concatenate([c_ctx[None], c], axis=0), ((0, rows - n_c), (0, 0)))
    mod = _mod_vectors(c_all, mod_w, mod_b).reshape(depth, rows, 6, D)

    rw_t = jnp.pad(router_w.T, ((0, LANE - N_EXPERTS), (0, 0)))
    rb_col = jnp.pad(router_bias, (0, LANE - N_EXPERTS))[:, None]
    tmp, tms = _tile(sp, 256), _tile(ss, 512)

    new_caches = []
    new_ssd = []
    for l in range(depth):
        i = l // 2
        modp = mod[l, 0:1]
        mods = mod[l, 1:1 + bs]
        g1 = norm1_g[l][None]
        g2 = norm2_g[l][None]
        if l % 2 == 0:
            lam_init = 0.8 - 0.6 * math.exp(-0.3 * l)
            wp, gbk = _attn_weights(attn_w_in[i], a_qk_g[i], b_qa_g[i], b_wq_up[i], b_kva_g[i], b_qk_g[i])
            ctx = (cache_a_k[:, i], cache_a_v[:, i], cache_b_ckv[:, i], cache_b_krope[:, i])
            xp, xs, caches = _attn_layer(xp, xs, shapes, modp, mods, g1, wp, gbk, b_wkv_up[i].astype(BF16),
                                         a_lambda[i], a_sub_g[i][None], attn_w_out[i].astype(BF16), ctx, lam_init)
            new_caches.append(caches)
        else:
            w_in = ssd_w_in[i]
            n0 = SSD_DI + SSD_CONV_DIM
            w_in_p = jnp.concatenate([w_in[:, :n0], _pad_chunks(w_in[:, n0:], 2, SSD_H, LANE)], axis=1).astype(BF16)
            bias = _pad_chunks(ssd_dt_bias[i].reshape(1, 2 * SSD_H), 2, SSD_H, LANE)
            alog = _pad_chunks(ssd_a_log[i].reshape(1, 2 * SSD_H), 2, SSD_H, LANE)
            dvec = jnp.repeat(ssd_d[i], SSD_P)[None]
            args = (g1, w_in_p, ssd_conv_w[i], ssd_conv_b[i][None], bias, alog, dvec, ssd_norm_g[i][None],
                    ssd_w_out[i].astype(BF16))
            zero = jnp.zeros((bp, SSD_DI, SSD_N), F32)
            xp, sf, sb = _ssd_stream(xp, modp, *args, zero, zero, bsz=bp, seq=sp, per_batch_mod=False, tm=tmp)
            xs, _, _ = _ssd_stream(xs, mods, *args, state_ssd[:, i, 0].reshape(bs, SSD_DI, SSD_N),
                                   state_ssd[:, i, 1].reshape(bs, SSD_DI, SSD_N), bsz=bs, seq=ss,
                                   per_batch_mod=True, tm=tms)
            new_ssd.append(jnp.stack([sf, sb], axis=1).reshape(bp, 2, SSD_H, SSD_P, SSD_N))
        win_g = jnp.concatenate(
            [moe_w_in[l].reshape(N_GROUPS, EPG, D, 2 * D_EXPERT).transpose(0, 2, 1, 3).reshape(N_GROUPS, D, -1),
             jnp.broadcast_to(shared_w_in[l][None], (N_GROUPS, D, 2 * D_EXPERT))], axis=2).astype(BF16)
        wout_g = jnp.concatenate(
            [moe_w_out[l].reshape(N_GROUPS, EPG * D_EXPERT, D),
             jnp.broadcast_to(shared_w_out[l][None], (N_GROUPS, D_EXPERT, D))], axis=1).astype(BF16)
        xp = _moe_block(xp, modp, g2, rw_t, rb_col, win_g, wout_g, seq=sp, per_batch_mod=False, tm=tmp)
        xs = _moe_block(xs, mods, g2, rw_t, rb_col, win_g, wout_g, seq=ss, per_batch_mod=True, tm=tms)

    cat = lambda k: jnp.concatenate([cc[k] for cc in new_caches], axis=1)
    return (xp.reshape(bp, sp, D), xs.reshape(bs, ss, D), cat(0), cat(1), cat(2), cat(3),
            jnp.stack(new_ssd, axis=1))
```

```python
import functools
import math

import jax
import jax.numpy as jnp
from jax import lax
from jax.experimental import pallas as pl
from jax.experimental.pallas import tpu as pltpu

F32 = jnp.float32
BF16 = jnp.bfloat16

D = 1024
EPS = 1e-6
GRID_W = 64
ROPE_BASE = 10000.0
LANE = 128

A_HEADS = 4
A_DQK = 64
A_DV = 128
B_HEADS = 4
B_NOPE = 128
B_ROPE = 64
B_DQK = B_NOPE + B_ROPE
B_DV = 128
B_Q_LORA = 384
B_KV_LORA = 256

SSD_DI = 2048
SSD_P = 64
SSD_H = 32
SSD_N = 128
SSD_G = 4
SSD_CHUNK = 128
SSD_CONV_DIM = SSD_DI + 2 * SSD_G * SSD_N

N_EXPERTS = 16
N_GROUPS = 4
EPG = 4
D_EXPERT = 256

VMEM_LIMIT = 56 * 1024 * 1024


def _cparams(*sem):
    return pltpu.CompilerParams(dimension_semantics=sem, vmem_limit_bytes=VMEM_LIMIT)


def _dot(a, b):
    return jnp.dot(a, b, preferred_element_type=F32)


def _dot_nt(a, b):
    return lax.dot_general(a, b, (((1,), (1,)), ((), ())), preferred_element_type=F32)


def _split3(x):
    x1 = x.astype(BF16)
    r = x - x1.astype(F32)
    x2 = r.astype(BF16)
    x3 = (r - x2.astype(F32)).astype(BF16)
    return x1, x2, x3


def _silu(x):
    return x * (1.0 / (1.0 + jnp.exp(-x)))


def _rms(x, n):
    return lax.rsqrt(jnp.sum(x * x, axis=-1, keepdims=True) * (1.0 / n) + EPS)


def _modulated(x, g, shift, scale):
    return x * _rms(x, D) * g * (1.0 + scale) + shift


def _mod_kernel(c_ref, w_ref, b_ref, o_ref):
    c = _silu(c_ref[...])
    w = w_ref[0]
    c1, c2, c3 = _split3(c)
    w1, w2, w3 = _split3(w)
    acc = _dot(c1, w1) + (_dot(c1, w2) + _dot(c2, w1)) + (_dot(c2, w2) + _dot(c1, w3) + _dot(c3, w1))
    o_ref[0] = acc + b_ref[0]


def _mod_vectors(c_all, mod_w, mod_b):
    depth = mod_w.shape[0]
    rows = c_all.shape[0]
    tn = 1536
    return pl.pallas_call(
        _mod_kernel,
        out_shape=jax.ShapeDtypeStruct((depth, rows, 6 * D), F32),
        grid=(depth, 6 * D // tn),
        in_specs=[
            pl.BlockSpec((rows, D), lambda l, j: (0, 0)),
            pl.BlockSpec((1, D, tn), lambda l, j: (l, 0, j)),
            pl.BlockSpec((1, 1, tn), lambda l, j: (l, 0, j)),
        ],
        out_specs=pl.BlockSpec((1, rows, tn), lambda l, j: (l, 0, j)),
        compiler_params=_cparams("parallel", "parallel"),
        name="mod_vectors",
    )(c_all, mod_w, mod_b.reshape(depth, 1, 6 * D))


def _rope_tables(n_tok):
    t = jnp.arange(n_tok)
    row = (t // GRID_W).astype(F32)
    col = (t % GRID_W).astype(F32)
    half = B_ROPE // 2
    inv = ROPE_BASE ** (-2.0 * jnp.arange(half // 2, dtype=F32) / half)
    ang = jnp.concatenate([row[:, None] * inv, col[:, None] * inv], axis=-1)
    cos, sin = jnp.cos(ang), jnp.sin(ang)
    zero = jnp.zeros((n_tok, LANE - 2 * half), F32)
    cos_t = jnp.concatenate([cos, cos, zero], axis=-1)
    sin_t = jnp.concatenate([-sin, sin, zero], axis=-1)
    return cos_t, sin_t


def _rope_chunk(c, cos, sin):
    lane = lax.broadcasted_iota(jnp.int32, c.shape, 1)
    swapped = jnp.where(lane < 32, pltpu.roll(c, LANE - 32, 1), pltpu.roll(c, 32, 1))
    return c * cos + swapped * sin


N_ATTN_COLS = 3328
_AQ0, _AK0, _AV0, _BQ0, _BKV0, _BKR0 = 0, 1024, 2048, 2560, 2944, 3200


def _attn_in_kernel(x_ref, mod_ref, g1_ref, w_ref, gq_ref, gk_ref, gqa_ref, wq_ref, gbq_ref, gkva_ref,
                    cos_ref, sin_ref, qa_ref, ka_ref, va_ref, qb_ref, ckv_ref, kr_ref, *, rope):
    x = x_ref[...]
    h = _modulated(x, g1_ref[...], mod_ref[0, 0:1, :], mod_ref[0, 1:2, :])
    z = _dot(h.astype(BF16), w_ref[...])
    if rope:
        cos = cos_ref[...]
        sin = sin_ref[...]
    for j in range(2 * A_HEADS):
        sl = slice(j * LANE, (j + 1) * LANE)
        q = z[:, _AQ0 + j * LANE:_AQ0 + (j + 1) * LANE]
        q = q * _rms(q, A_DQK) * gq_ref[:, sl]
        k = z[:, _AK0 + j * LANE:_AK0 + (j + 1) * LANE]
        k = k * _rms(k, A_DQK) * gk_ref[:, sl]
        if rope:
            q = _rope_chunk(q, cos, sin)
            k = _rope_chunk(k, cos, sin)
        qa_ref[:, sl] = q.astype(qa_ref.dtype)
        ka_ref[:, sl] = k.astype(ka_ref.dtype)
    va_ref[...] = z[:, _AV0:_AV0 + A_HEADS * A_DV].astype(va_ref.dtype)
    qc = z[:, _BQ0:_BQ0 + B_Q_LORA]
    qc = qc * _rms(qc, B_Q_LORA) * gqa_ref[...]
    qb = _dot(qc.astype(BF16), wq_ref[...])
    for hh in range(B_HEADS):
        lo = qb[:, hh * 256:hh * 256 + LANE]
        hi = qb[:, hh * 256 + LANE:(hh + 1) * 256]
        inv = lax.rsqrt((jnp.sum(lo * lo, axis=-1, keepdims=True)
                         + jnp.sum(hi * hi, axis=-1, keepdims=True)) * (1.0 / B_DQK) + EPS)
        lo = lo * inv * gbq_ref[:, hh * 256:hh * 256 + LANE]
        hi = hi * inv * gbq_ref[:, hh * 256 + LANE:(hh + 1) * 256]
        if rope:
            hi = _rope_chunk(hi, cos, sin)
        qb_ref[:, hh * 256:hh * 256 + LANE] = lo.astype(qb_ref.dtype)
        qb_ref[:, hh * 256 + LANE:(hh + 1) * 256] = hi.astype(qb_ref.dtype)
    kvc = z[:, _BKV0:_BKV0 + B_KV_LORA]
    ckv_ref[...] = kvc * _rms(kvc, B_KV_LORA) * gkva_ref[...]
    kr_ref[...] = z[:, _BKR0:_BKR0 + LANE]


def _attn_in(x, mod, g1, wp, rope_tabs, *, seq, per_batch_mod, rope, kv_dtype, tm):
    t = x.shape[0]
    steps_per_seq = seq // tm
    mod_map = (lambda i: (i // steps_per_seq, 0, 0)) if per_batch_mod else (lambda i: (0, 0, 0))
    tab_map = lambda i: (i % steps_per_seq, 0)
    row = lambda n: pl.BlockSpec((tm, n), lambda i: (i, 0))
    full = lambda a: pl.BlockSpec(a.shape, lambda i: (0,) * a.ndim)
    cos_t, sin_t = rope_tabs
    ins = [x, mod, g1, wp["w_in"], wp["gq"], wp["gk"], wp["gqa"], wp["wq"], wp["gbq"], wp["gkva"], cos_t, sin_t]
    in_specs = [row(D), pl.BlockSpec((1, 6, D), mod_map), full(g1), full(wp["w_in"]), full(wp["gq"]),
                full(wp["gk"]), full(wp["gqa"]), full(wp["wq"]), full(wp["gbq"]), full(wp["gkva"]),
                pl.BlockSpec((tm, LANE), tab_map), pl.BlockSpec((tm, LANE), tab_map)]
    out_shape = [
        jax.ShapeDtypeStruct((t, 1024), BF16),
        jax.ShapeDtypeStruct((t, 1024), kv_dtype),
        jax.ShapeDtypeStruct((t, 512), kv_dtype),
        jax.ShapeDtypeStruct((t, 1024), BF16),
        jax.ShapeDtypeStruct((t, B_KV_LORA), F32),
        jax.ShapeDtypeStruct((t, LANE), F32),
    ]
    out_specs = [row(1024), row(1024), row(512), row(1024), row(B_KV_LORA), row(LANE)]
    return pl.pallas_call(
        functools.partial(_attn_in_kernel, rope=rope),
        out_shape=out_shape, grid=(t // tm,), in_specs=in_specs, out_specs=out_specs,
        compiler_params=_cparams("parallel"), name="attn_in",
    )(*ins)


def _mla_expand_kernel(ckv_ref, kr_ref, w_ref, g_ref, cos_ref, sin_ref, kb_ref, vb_ref, *, rope):
    kv = _dot(ckv_ref[...].astype(BF16), w_ref[...])
    kr = kr_ref[...]
    kr_ss = jnp.sum(kr * kr, axis=-1, keepdims=True)
    for hh in range(B_HEADS):
        kn = kv[:, hh * 256:hh * 256 + LANE]
        inv = lax.rsqrt((jnp.sum(kn * kn, axis=-1, keepdims=True) + kr_ss) * (1.0 / B_DQK) + EPS)
        hi = kr * inv * g_ref[:, LANE:]
        if rope:
            hi = _rope_chunk(hi, cos_ref[...], sin_ref[...])
        kb_ref[:, hh * 256:hh * 256 + LANE] = (kn * inv * g_ref[:, :LANE]).astype(kb_ref.dtype)
        kb_ref[:, hh * 256 + LANE:(hh + 1) * 256] = hi.astype(kb_ref.dtype)
        vb_ref[:, hh * LANE:(hh + 1) * LANE] = kv[:, hh * 256 + LANE:(hh + 1) * 256].astype(vb_ref.dtype)


def _mla_expand(ckv, kr, wkv, gk, rope_tabs, *, seq, rope, tm):
    t = ckv.shape[0]
    steps_per_seq = seq // tm
    tab_map = lambda i: (i % steps_per_seq, 0)
    row = lambda n: pl.BlockSpec((tm, n), lambda i: (i, 0))
    full = lambda a: pl.BlockSpec(a.shape, lambda i: (0,) * a.ndim)
    cos_t, sin_t = rope_tabs
    return pl.pallas_call(
        functools.partial(_mla_expand_kernel, rope=rope),
        out_shape=[jax.ShapeDtypeStruct((t, 1024), BF16), jax.ShapeDtypeStruct((t, 512), BF16)],
        grid=(t // tm,),
        in_specs=[row(B_KV_LORA), row(LANE), full(wkv), full(gk),
                  pl.BlockSpec((tm, LANE), tab_map), pl.BlockSpec((tm, LANE), tab_map)],
        out_specs=[row(1024), row(512)],
        compiler_params=_cparams("parallel"), name="mla_expand",
    )(ckv, kr, wkv, gk, cos_t, sin_t)


VT_ROWS = A_DV + 16


def _values_t(va, vb):
    b, sk, _ = va.shape
    v = jnp.concatenate([va.reshape(b, sk, A_HEADS, A_DV), vb.reshape(b, sk, B_HEADS, B_DV)], axis=2)
    v = jnp.transpose(v.astype(BF16), (0, 2, 3, 1))
    return jnp.concatenate([v, jnp.ones((b, A_HEADS + B_HEADS, VT_ROWS - A_DV, sk), BF16)], axis=2)


def _attn_kernel(*refs, n_pieces, lam_init):
    qa_ref, qb_ref, lam_ref, gsub_ref = refs[:4]
    kv_refs = refs[4:4 + 3 * n_pieces]
    o_ref, s_scr, p_scr = refs[4 + 3 * n_pieces:]
    ka = kv_refs[0::3]
    kb = kv_refs[1::3]
    vt = kv_refs[2::3]
    lam_p = lam_ref[...]
    lam = (jnp.exp(jnp.sum(lam_p[0:1] * lam_p[1:2], axis=-1, keepdims=True))
           - jnp.exp(jnp.sum(lam_p[2:3] * lam_p[3:4], axis=-1, keepdims=True)) + lam_init)
    n_maps = 2 * A_HEADS + B_HEADS

    def scores(j):
        r0 = 0
        for i in range(n_pieces):
            if j < 2 * A_HEADS:
                sl = slice(j * LANE, (j + 1) * LANE)
                s = _dot_nt(ka[i][0, :, sl].astype(BF16), qa_ref[0, :, sl])
            else:
                sl = slice((j - 2 * A_HEADS) * 256, (j - 2 * A_HEADS + 1) * 256)
                s = _dot_nt(kb[i][0, :, sl], qb_ref[0, :, sl])
            s_scr[j % 2, r0:r0 + s.shape[0], :] = s
            r0 += s.shape[0]

    def attend(j):
        head = j // 2 if j < 2 * A_HEADS else j - A_HEADS
        mx = jnp.max(s_scr[j % 2], axis=0, keepdims=True)
        p_scr[j % 2] = jnp.exp2(s_scr[j % 2] - mx).astype(BF16)
        acc = None
        r0 = 0
        for i in range(n_pieces):
            v = vt[i][0, head]
            part = _dot(v, p_scr[j % 2, r0:r0 + v.shape[1], :])
            acc = part if acc is None else acc + part
            r0 += v.shape[1]
        return acc[:A_DV] / acc[A_DV:A_DV + 1]

    scores(0)
    first = None
    for j in range(n_maps):
        if j + 1 < n_maps:
            scores(j + 1)
        out = attend(j)
        if j >= 2 * A_HEADS:
            c0 = A_HEADS * A_DV + (j - 2 * A_HEADS) * B_DV
            o_ref[0, :, c0:c0 + B_DV] = jnp.transpose(out).astype(o_ref.dtype)
        elif j % 2 == 0:
            first = out
        else:
            o = jnp.transpose(first - lam * out)
            o = o * _rms(o, A_DV) * gsub_ref[...] * (1.0 - lam_init)
            o_ref[0, :, (j // 2) * A_DV:(j // 2 + 1) * A_DV] = o.astype(o_ref.dtype)


def _attention(qa, qb, lam_p, gsub, pieces, *, lam_init, tq):
    b, s, _ = qa.shape
    qspec = pl.BlockSpec((1, tq, 1024), lambda bi, qi: (bi, qi, 0))
    full = lambda a: pl.BlockSpec(a.shape, lambda bi, qi: (0,) * a.ndim)
    ins = [qa, qb, lam_p, gsub]
    in_specs = [qspec, qspec, full(lam_p), full(gsub)]
    for piece in pieces:
        for a in piece:
            ins.append(a)
            in_specs.append(pl.BlockSpec((1,) + a.shape[1:], lambda bi, qi, nd=a.ndim: (bi,) + (0,) * (nd - 1),
                                         pipeline_mode=pl.Buffered(1)))
    sk = sum(piece[0].shape[1] for piece in pieces)
    return pl.pallas_call(
        functools.partial(_attn_kernel, n_pieces=len(pieces), lam_init=lam_init),
        out_shape=jax.ShapeDtypeStruct((b, s, 1024), BF16),
        grid=(b, s // tq), in_specs=in_specs,
        out_specs=pl.BlockSpec((1, tq, 1024), lambda bi, qi: (bi, qi, 0)),
        scratch_shapes=[pltpu.VMEM((2, sk, tq), F32), pltpu.VMEM((2, sk, tq), BF16)],
        compiler_params=_cparams("parallel", "arbitrary"), name="attn_core",
    )(*ins)


def _proj_res_kernel(o_ref, x_ref, mod_ref, w_ref, y_ref, *, gate_row):
    y_ref[...] = x_ref[...] + mod_ref[0, gate_row:gate_row + 1, :] * _dot(o_ref[...], w_ref[...])


def _proj_res(o, x, mod, w, *, seq, per_batch_mod, gate_row, tm):
    t, k = o.shape
    steps_per_seq = seq // tm
    mod_map = (lambda i: (i // steps_per_seq, 0, 0)) if per_batch_mod else (lambda i: (0, 0, 0))
    return pl.pallas_call(
        functools.partial(_proj_res_kernel, gate_row=gate_row),
        out_shape=jax.ShapeDtypeStruct((t, D), F32), grid=(t // tm,),
        in_specs=[pl.BlockSpec((tm, k), lambda i: (i, 0)), pl.BlockSpec((tm, D), lambda i: (i, 0)),
                  pl.BlockSpec((1, 6, D), mod_map), pl.BlockSpec(w.shape, lambda i: (0, 0))],
        out_specs=pl.BlockSpec((tm, D), lambda i: (i, 0)),
        compiler_params=_cparams("parallel"), name="proj_res",
    )(o, x, mod, w)


def _route(logits_t, bias_col):
    tm = logits_t.shape[1]
    scores = 1.0 / (1.0 + jnp.exp(-logits_t[:N_EXPERTS]))
    sel = scores + bias_col[:N_EXPERTS]
    grp = []
    for g in range(N_GROUPS):
        r = [sel[EPG * g + j:EPG * g + j + 1] for j in range(EPG)]
        best2 = None
        for a in range(EPG):
            for b2 in range(a + 1, EPG):
                pair = r[a] + r[b2]
                best2 = pair if best2 is None else jnp.maximum(best2, pair)
        grp.append(best2)
    best = jnp.zeros((1, tm), jnp.int32)
    best_v = grp[0]
    for g in range(1, N_GROUPS):
        better = grp[g] > best_v
        best = jnp.where(better, g, best)
        best_v = jnp.where(better, grp[g], best_v)
    v = []
    sc = []
    for j in range(EPG):
        vj = sel[j:j + 1]
        sj = scores[j:j + 1]
        for g in range(1, N_GROUPS):
            vj = jnp.where(best == g, sel[EPG * g + j:EPG * g + j + 1], vj)
            sj = jnp.where(best == g, scores[EPG * g + j:EPG * g + j + 1], sj)
        v.append(vj)
        sc.append(sj)
    picked = []
    for j in range(EPG):
        rank = jnp.zeros((1, tm), jnp.int32)
        for k in range(EPG):
            if k == j:
                continue
            ahead = (v[k] >= v[j]) if k < j else (v[k] > v[j])
            rank = rank + ahead.astype(jnp.int32)
        picked.append(rank < 2)
    den = functools.reduce(jnp.add, [jnp.where(picked[j], sc[j], 0.0) for j in range(EPG)])
    gates = [jnp.where(picked[j], sc[j] / den, 0.0) for j in range(EPG)]
    return best, gates


def _moe_route_kernel(x_ref, mod_ref, g2_ref, rw_ref, rb_ref, h_ref, info_ref, carry):
    tm = x_ref.shape[0]

    @pl.when(pl.program_id(0) == 0)
    def _():
        carry[...] = jnp.zeros_like(carry)

    h = _modulated(x_ref[...], g2_ref[...], mod_ref[0, 3:4, :], mod_ref[0, 4:5, :])
    h_ref[...] = h
    h1, h2, h3 = _split3(h)
    w1, w2, w3 = _split3(rw_ref[...])
    logits_t = (_dot_nt(w1, h1) + (_dot_nt(w1, h2) + _dot_nt(w2, h1))
                + (_dot_nt(w2, h2) + _dot_nt(w1, h3) + _dot_nt(w3, h1)))
    best, gates = _route(logits_t, rb_ref[...])
    grow = lax.broadcasted_iota(jnp.int32, (8, tm), 0)
    onehot = jnp.where(grow == best, 1.0, 0.0)
    ti = lax.broadcasted_iota(jnp.int32, (tm, tm), 0)
    tj = lax.broadcasted_iota(jnp.int32, (tm, tm), 1)
    upper = jnp.where(ti <= tj, 1.0, 0.0).astype(BF16)
    cum = _dot(onehot.astype(BF16), upper) + carry[:, 0:1]
    rank = jnp.sum(onehot * cum, axis=0, keepdims=True) - 1.0
    carry[...] = jnp.broadcast_to(cum[:, tm - 1:tm], carry.shape)
    info_ref[...] = jnp.concatenate(gates + [best.astype(F32), rank, jnp.zeros((2, tm), F32)], axis=0)


def _moe_route(x, mod, g2, rw_t, rb_col, *, seq, per_batch_mod, tm):
    t = x.shape[0]
    steps_per_seq = seq // tm
    mod_map = (lambda i: (i // steps_per_seq, 0, 0)) if per_batch_mod else (lambda i: (0, 0, 0))
    full = lambda a: pl.BlockSpec(a.shape, lambda i: (0,) * a.ndim)
    return pl.pallas_call(
        _moe_route_kernel,
        out_shape=[jax.ShapeDtypeStruct((t, D), F32), jax.ShapeDtypeStruct((8, t), F32)],
        grid=(t // tm,),
        in_specs=[pl.BlockSpec((tm, D), lambda i: (i, 0)), pl.BlockSpec((1, 6, D), mod_map), full(g2),
                  full(rw_t), full(rb_col)],
        out_specs=[pl.BlockSpec((tm, D), lambda i: (i, 0)), pl.BlockSpec((8, tm), lambda i: (0, i))],
        scratch_shapes=[pltpu.VMEM((8, LANE), F32)],
        compiler_params=_cparams("arbitrary"), name="moe_route",
    )(x, mod, g2, rw_t, rb_col)


def _dispatch_plan(info, t, tm):
    grp = info[4].astype(jnp.int32)
    rank = info[5].astype(jnp.int32)
    n_slots = t + N_GROUPS * tm
    n_tiles = n_slots // tm
    counts = [jnp.sum((grp == g).astype(jnp.int32)) for g in range(N_GROUPS)]
    tiles = [(c + tm - 1) // tm for c in counts]
    start_tile = [sum(tiles[:g], jnp.int32(0)) for g in range(N_GROUPS + 1)]
    dest = rank + sum(jnp.where(grp == g, start_tile[g] * tm, 0) for g in range(N_GROUPS))
    n_fill = n_slots - t
    k = jnp.arange(n_fill, dtype=jnp.int32)
    fill_key = jnp.zeros_like(k)
    seen = jnp.int32(0)
    for g in range(N_GROUPS):
        pad = tiles[g] * tm - counts[g]
        fill_key = jnp.where((k >= seen) & (k < seen + pad), start_tile[g] * tm + counts[g] + (k - seen), fill_key)
        seen = seen + pad
    fill_key = jnp.where(k >= seen, start_tile[N_GROUPS] * tm + (k - seen), fill_key)
    keys = jnp.concatenate([dest, fill_key])
    tok = jnp.concatenate([jnp.arange(t, dtype=jnp.int32), jnp.full((n_fill,), t, jnp.int32)])
    gate_rows = [jnp.concatenate([info[j], jnp.zeros((n_fill,), F32)]) for j in range(EPG)]
    _, tok_s, *gate_s = lax.sort((keys, tok, *gate_rows), num_keys=1)
    valid = tok_s < t
    slot = jnp.arange(n_slots, dtype=jnp.int32)
    dump = t + ((slot // tm) % 2) * tm + slot % tm
    src = jnp.where(valid, tok_s, 0)
    dst = jnp.where(valid, tok_s, dump)
    tile_idx = jnp.arange(n_tiles, dtype=jnp.int32)
    tile_group = sum((tile_idx >= start_tile[g]).astype(jnp.int32) for g in range(1, N_GROUPS))
    return (tile_group, src.reshape(n_tiles, 1, tm), dst.reshape(n_tiles, 1, tm), jnp.stack(gate_s, axis=1))


def _moe_sorted_kernel(tg_ref, src_ref, srcn_ref, dst_ref, gate_ref, win_ref, wout_ref, shin_ref, shout_ref,
                       h_hbm, y_hbm, hbuf, ybuf, win_bf, wout_bf, gsem, ssem, *, tm, n_tiles):
    i = pl.program_id(0)
    slot = lax.rem(i, 2)

    def gather(idx_ref, s):
        for r in range(tm):
            pltpu.make_async_copy(h_hbm.at[pl.ds(idx_ref[0, 0, r], 1)], hbuf.at[s, pl.ds(r, 1)],
                                  gsem.at[s]).start(priority=r % 2)

    def gather_wait(s):
        pltpu.make_async_copy(h_hbm.at[pl.ds(0, tm)], hbuf.at[s], gsem.at[s]).wait()

    def scatter_wait(s):
        pltpu.make_async_copy(ybuf.at[s], y_hbm.at[pl.ds(0, tm)], ssem.at[s]).wait()

    @pl.when(i == 0)
    def _():
        gather(src_ref, 0)

    @pl.when(i + 1 < n_tiles)
    def _():
        gather(srcn_ref, 1 - slot)

    @pl.when(i == 0)
    def _():
        win_bf[EPG] = shin_ref[...].astype(BF16)
        wout_bf[EPG * D_EXPERT:, :] = shout_ref[...].astype(BF16)

    @pl.when((i == 0) | (tg_ref[i] != tg_ref[jnp.maximum(i - 1, 0)]))
    def _():
        for e in range(EPG):
            win_bf[e] = win_ref[e].astype(BF16)
            wout_bf[e * D_EXPERT:(e + 1) * D_EXPERT, :] = wout_ref[e].astype(BF16)

    gather_wait(slot)
    x = hbuf[slot].astype(BF16)
    parts = []
    for e in range(EPG + 1):
        ue = _dot(x, win_bf[e])
        he = _silu(ue[:, :D_EXPERT]) * ue[:, D_EXPERT:]
        if e < EPG:
            he = he * gate_ref[:, e:e + 1]
        parts.append(he.astype(BF16))
    y = _dot(jnp.concatenate(parts, axis=1), wout_bf[...])

    @pl.when(i >= 2)
    def _():
        scatter_wait(slot)

    ybuf[slot] = y
    for r in range(tm):
        pltpu.make_async_copy(ybuf.at[slot, pl.ds(r, 1)], y_hbm.at[pl.ds(dst_ref[0, 0, r], 1)],
                              ssem.at[slot]).start(priority=r % 2)

    @pl.when(i == n_tiles - 1)
    def _():
        if n_tiles >= 2:
            scatter_wait(1 - slot)
        scatter_wait(slot)


def _moe_sorted(h, plan, layer, w_in, w_out, sh_in, sh_out, *, tm):
    t = h.shape[0]
    tile_group, src, dst, gates = plan
    n_tiles = src.shape[0]
    smem_blk = lambda imap: pl.BlockSpec((1, 1, tm), imap, memory_space=pltpu.SMEM)
    grid_spec = pltpu.PrefetchScalarGridSpec(
        num_scalar_prefetch=1, grid=(n_tiles,),
        in_specs=[smem_blk(lambda i, tg: (i, 0, 0)),
                  smem_blk(lambda i, tg: (jnp.minimum(i + 1, n_tiles - 1), 0, 0)),
                  smem_blk(lambda i, tg: (i, 0, 0)),
                  pl.BlockSpec((tm, EPG), lambda i, tg: (i, 0)),
                  pl.BlockSpec((None, EPG, D, 2 * D_EXPERT), lambda i, tg: (layer, tg[i], 0, 0)),
                  pl.BlockSpec((None, EPG, D_EXPERT, D), lambda i, tg: (layer, tg[i], 0, 0)),
                  pl.BlockSpec((None, D, 2 * D_EXPERT), lambda i, tg: (layer, 0, 0)),
                  pl.BlockSpec((None, D_EXPERT, D), lambda i, tg: (layer, 0, 0)),
                  pl.BlockSpec(memory_space=pl.ANY)],
        out_specs=pl.BlockSpec(memory_space=pl.ANY),
        scratch_shapes=[pltpu.VMEM((2, tm, D), F32), pltpu.VMEM((2, tm, D), F32),
                        pltpu.VMEM((EPG + 1, D, 2 * D_EXPERT), BF16), pltpu.VMEM(((EPG + 1) * D_EXPERT, D), BF16),
                        pltpu.SemaphoreType.DMA((2,)), pltpu.SemaphoreType.DMA((2,))])
    return pl.pallas_call(
        functools.partial(_moe_sorted_kernel, tm=tm, n_tiles=n_tiles),
        out_shape=jax.ShapeDtypeStruct((t + 2 * tm, D), F32), grid_spec=grid_spec,
        compiler_params=_cparams("arbitrary"), name="moe_sorted",
    )(tile_group, src, src, dst, gates, w_in, w_out, sh_in, sh_out, h)


def _moe_residual_kernel(x_ref, y_ref, mod_ref, o_ref):
    o_ref[...] = x_ref[...] + mod_ref[0, 5:6, :] * y_ref[...]


def _moe_residual(x, y, mod, *, seq, per_batch_mod, tm):
    t = x.shape[0]
    steps_per_seq = seq // tm
    mod_map = (lambda i: (i // steps_per_seq, 0, 0)) if per_batch_mod else (lambda i: (0, 0, 0))
    row = pl.BlockSpec((tm, D), lambda i: (i, 0))
    return pl.pallas_call(
        _moe_residual_kernel, out_shape=jax.ShapeDtypeStruct((t, D), F32), grid=(t // tm,),
        in_specs=[row, row, pl.BlockSpec((1, 6, D), mod_map)], out_specs=row,
        compiler_params=_cparams("parallel"), name="moe_residual",
    )(x, y, mod)


MOE_TILE = 256


def _moe_block(x, mod, g2, rw_t, rb_col, expert_w, *, seq, per_batch_mod, tm):
    h, info = _moe_route(x, mod, g2, rw_t, rb_col, seq=seq, per_batch_mod=per_batch_mod, tm=tm)
    plan = _dispatch_plan(info, x.shape[0], MOE_TILE)
    y = _moe_sorted(h, plan, *expert_w, tm=MOE_TILE)
    return _moe_residual(x, y, mod, seq=seq, per_batch_mod=per_batch_mod, tm=tm)


SSD_IN_COLS = SSD_DI + SSD_CONV_DIM + 2 * LANE


def _ssd_in_kernel(x_ref, mod_ref, g1_ref, w_ref, z_ref, xbc_ref, dt_ref):
    h = _modulated(x_ref[...], g1_ref[...], mod_ref[0, 0:1, :], mod_ref[0, 1:2, :])
    hb = h.astype(BF16)
    z_ref[...] = _dot(hb, w_ref[:, :SSD_DI]).astype(z_ref.dtype)
    xbc_ref[...] = _dot(hb, w_ref[:, SSD_DI:SSD_DI + SSD_CONV_DIM]).astype(xbc_ref.dtype)
    dt_ref[...] = _dot(hb, w_ref[:, SSD_DI + SSD_CONV_DIM:])


def _ssd_in(x, mod, g1, w, *, seq, per_batch_mod, tm):
    t = x.shape[0]
    steps_per_seq = seq // tm
    mod_map = (lambda i: (i // steps_per_seq, 0, 0)) if per_batch_mod else (lambda i: (0, 0, 0))
    row = lambda n: pl.BlockSpec((tm, n), lambda i: (i, 0))
    return pl.pallas_call(
        _ssd_in_kernel,
        out_shape=[jax.ShapeDtypeStruct((t, SSD_DI), BF16), jax.ShapeDtypeStruct((t, SSD_CONV_DIM), F32),
                   jax.ShapeDtypeStruct((t, 2 * LANE), F32)],
        grid=(t // tm,),
        in_specs=[row(D), pl.BlockSpec((1, 6, D), mod_map), pl.BlockSpec(g1.shape, lambda i: (0, 0)),
                  pl.BlockSpec(w.shape, lambda i: (0, 0))],
        out_specs=[row(SSD_DI), row(SSD_CONV_DIM), row(2 * LANE)],
        compiler_params=_cparams("parallel"), name="ssd_in",
    )(x, mod, g1, w)


def _conv_kernel(x_ref, w_ref, b_ref, o_ref):
    x = x_ref[0]
    s = x.shape[0]

    def taps(prev, cur, nxt):
        y = prev * w_ref[0:1, :] + cur * w_ref[1:2, :] + nxt * w_ref[2:3, :] + b_ref[...]
        return _silu(y).astype(o_ref.dtype)

    o_ref[0] = taps(pltpu.roll(x, 1, 0), x, pltpu.roll(x, s - 1, 0))
    rowi = lax.broadcasted_iota(jnp.int32, (32, x.shape[1]), 0)
    top = x[0:32]
    o_ref[0, 0:16, :] = taps(jnp.where(rowi == 0, 0.0, pltpu.roll(top, 1, 0)), top, pltpu.roll(top, 31, 0))[0:16]
    bot = x[s - 32:s]
    o_ref[0, s - 16:s, :] = taps(pltpu.roll(bot, 1, 0), bot,
                                 jnp.where(rowi == 31, 0.0, pltpu.roll(bot, 31, 0)))[16:32]


def _conv_silu(xbc, w, b, *, cw):
    bsz, s, c = xbc.shape
    return pl.pallas_call(
        _conv_kernel,
        out_shape=jax.ShapeDtypeStruct((bsz, s, c), BF16), grid=(bsz, c // cw),
        in_specs=[pl.BlockSpec((1, s, cw), lambda bi, j: (bi, 0, j)), pl.BlockSpec((3, cw), lambda bi, j: (0, j)),
                  pl.BlockSpec((1, cw), lambda bi, j: (0, j))],
        out_specs=pl.BlockSpec((1, s, cw), lambda bi, j: (bi, 0, j)),
        compiler_params=_cparams("parallel", "parallel"), name="conv_silu",
    )(xbc, w, b)


def _head_expander():
    return (jnp.arange(SSD_DI)[None, :] // SSD_P == jnp.arange(LANE)[:, None]).astype(BF16)


def _ssd_scan_kernel(*refs, reverse, add_skip):
    if add_skip:
        (x_ref, b_ref, c_ref, dt_ref, bias_ref, alog_ref, s0_ref, ex_ref, d_ref, yin_ref,
         y_ref, sfin_ref, state_t) = refs
    else:
        x_ref, b_ref, c_ref, dt_ref, bias_ref, alog_ref, s0_ref, ex_ref, y_ref, sfin_ref, state_t = refs
    ci = pl.program_id(1)
    q = SSD_CHUNK

    @pl.when(ci == 0)
    def _():
        state_t[...] = jnp.transpose(s0_ref[0])

    dtr = dt_ref[0] + bias_ref[...]
    dt = jnp.maximum(dtr, 0.0) + jnp.log(1.0 + jnp.exp(-jnp.abs(dtr)))
    a = dt * (-jnp.exp(alog_ref[...]))
    li = lax.broadcasted_iota(jnp.int32, (q, q), 0)
    si = lax.broadcasted_iota(jnp.int32, (q, q), 1)
    seen = (si >= li) if reverse else (si <= li)
    tri = jnp.where(seen, 1.0, 0.0).astype(BF16)
    a1, a2, a3 = _split3(a)
    cs = _dot(tri, a1) + _dot(tri, a2) + _dot(tri, a3)
    cs_t = jnp.transpose(cs)
    last = 0 if reverse else q - 1
    tot = cs[last:last + 1, :]
    dt_t = jnp.transpose(dt)
    dec = jnp.broadcast_to(jnp.exp(tot), (8, LANE))
    dec_hi = dec.astype(BF16)
    dec_lo = (dec - dec_hi.astype(F32)).astype(BF16)
    fac = jnp.concatenate([(dt * jnp.exp(tot - cs)).astype(BF16), jnp.exp(cs).astype(BF16), dec_hi, dec_lo], axis=0)
    fac = _dot(fac, ex_ref[...])
    dec_x = fac[2 * q:2 * q + 1, :] + fac[2 * q + 8:2 * q + 9, :]
    low = lax.broadcasted_iota(jnp.int32, (q, LANE), 1) < SSD_P
    for g in range(SSD_G):
        bg = b_ref[0, :, g * SSD_N:(g + 1) * SSD_N]
        cg = c_ref[0, :, g * SSD_N:(g + 1) * SSD_N]
        cb = _dot_nt(cg, bg)
        s_g = state_t[:, g * 512:(g + 1) * 512]
        y_off = _dot(cg, s_g.astype(BF16))
        xw_tiles = []
        for jj in range(4):
            j = g * 4 + jj
            cols = slice(j * LANE, (j + 1) * LANE)
            xb = x_ref[0, :, cols]
            ms = []
            for h in (2 * j, 2 * j + 1):
                seg = jnp.where(seen, cs[:, h:h + 1] - cs_t[h:h + 1, :], -jnp.inf)
                ms.append((cb * jnp.exp(seg) * dt_t[h:h + 1, :]).astype(BF16))
            zero = jnp.zeros_like(xb)
            rhs = jnp.concatenate([jnp.where(low, xb, zero), jnp.where(low, zero, xb)], axis=0)
            yt = y_off[:, jj * LANE:(jj + 1) * LANE] * fac[q:2 * q, cols] + _dot(jnp.concatenate(ms, axis=1), rhs)
            xt = xb.astype(F32)
            if add_skip:
                yt = yt + yin_ref[0, :, cols] + xt * d_ref[:, cols]
            y_ref[0, :, cols] = yt.astype(y_ref.dtype)
            xw_tiles.append((xt * fac[0:q, cols]).astype(BF16))
        xw = jnp.concatenate(xw_tiles, axis=1)
        bg_t = jnp.transpose(bg.astype(F32)).astype(BF16)
        state_t[:, g * 512:(g + 1) * 512] = s_g * dec_x[:, g * 512:(g + 1) * 512] + _dot(bg_t, xw)

    @pl.when(ci == pl.num_programs(1) - 1)
    def _():
        sfin_ref[0] = jnp.transpose(state_t[...])


def _ssd_scan(xbc, dt, bias, alog, s0, skip=None, *, reverse):
    bsz, s, _ = xbc.shape
    nc = s // SSD_CHUNK
    d = 1 if reverse else 0
    cmap = (lambda ci: nc - 1 - ci) if reverse else (lambda ci: ci)
    add_skip = skip is not None
    ins = [xbc, xbc, xbc, dt, bias, alog, s0, _head_expander()] + (list(skip) if add_skip else [])
    in_specs = [
        pl.BlockSpec((1, SSD_CHUNK, SSD_DI), lambda bi, ci: (bi, cmap(ci), 0)),
        pl.BlockSpec((1, SSD_CHUNK, 512), lambda bi, ci: (bi, cmap(ci), 4)),
        pl.BlockSpec((1, SSD_CHUNK, 512), lambda bi, ci: (bi, cmap(ci), 5)),
        pl.BlockSpec((1, SSD_CHUNK, LANE), lambda bi, ci: (bi, cmap(ci), d)),
        pl.BlockSpec((1, LANE), lambda bi, ci: (0, d)),
        pl.BlockSpec((1, LANE), lambda bi, ci: (0, d)),
        pl.BlockSpec((1, SSD_DI, SSD_N), lambda bi, ci: (bi, 0, 0)),
        pl.BlockSpec((LANE, SSD_DI), lambda bi, ci: (0, 0)),
    ]
    if add_skip:
        in_specs += [pl.BlockSpec((1, SSD_DI), lambda bi, ci: (0, 0)),
                     pl.BlockSpec((1, SSD_CHUNK, SSD_DI), lambda bi, ci: (bi, cmap(ci), 0))]
    return pl.pallas_call(
        functools.partial(_ssd_scan_kernel, reverse=reverse, add_skip=add_skip),
        out_shape=[jax.ShapeDtypeStruct((bsz, s, SSD_DI), F32), jax.ShapeDtypeStruct((bsz, SSD_DI, SSD_N), F32)],
        grid=(bsz, nc), in_specs=in_specs,
        out_specs=[pl.BlockSpec((1, SSD_CHUNK, SSD_DI), lambda bi, ci: (bi, cmap(ci), 0)),
                   pl.BlockSpec((1, SSD_DI, SSD_N), lambda bi, ci: (bi, 0, 0))],
        scratch_shapes=[pltpu.VMEM((SSD_N, SSD_DI), F32)],
        compiler_params=_cparams("parallel", "arbitrary"), name="ssd_scan_bwd" if reverse else "ssd_scan_fwd",
    )(*ins)


def _ssd_out_kernel(y_ref, z_ref, x_ref, mod_ref, ng_ref, w_ref, o_ref):
    y = y_ref[...] * _silu(z_ref[...].astype(F32))
    gw = SSD_DI // SSD_G
    parts = []
    for g in range(SSD_G):
        yg = y[:, g * gw:(g + 1) * gw]
        parts.append((yg * _rms(yg, gw) * ng_ref[:, g * gw:(g + 1) * gw]).astype(BF16))
    yn = jnp.concatenate(parts, axis=1)
    o_ref[...] = x_ref[...] + mod_ref[0, 2:3, :] * _dot(yn, w_ref[...])


def _ssd_out(y, z, x, mod, ng, w, *, seq, per_batch_mod, tm):
    t = x.shape[0]
    steps_per_seq = seq // tm
    mod_map = (lambda i: (i // steps_per_seq, 0, 0)) if per_batch_mod else (lambda i: (0, 0, 0))
    row = lambda n: pl.BlockSpec((tm, n), lambda i: (i, 0))
    return pl.pallas_call(
        _ssd_out_kernel,
        out_shape=jax.ShapeDtypeStruct((t, D), F32), grid=(t // tm,),
        in_specs=[row(SSD_DI), row(SSD_DI), row(D), pl.BlockSpec((1, 6, D), mod_map),
                  pl.BlockSpec(ng.shape, lambda i: (0, 0)), pl.BlockSpec(w.shape, lambda i: (0, 0))],
        out_specs=row(D),
        compiler_params=_cparams("parallel"), name="ssd_out",
    )(y, z, x, mod, ng, w)


def _pad_chunks(w, n_chunks, width, to):
    lead = w.shape[:-1]
    w = w.reshape(lead + (n_chunks, width))
    w = jnp.pad(w, [(0, 0)] * len(lead) + [(0, 0), (0, to - width)])
    return w.reshape(lead + (n_chunks * to,))


def _attn_weights(w_in, a_qk_g, b_qa_g, b_wq_up, b_kva_g, b_qk_g):
    c = [0, 512, 1024, 1536, 1920, 2176, 2240]
    w_in_p = jnp.concatenate([
        _pad_chunks(w_in[:, c[0]:c[1]], 8, A_DQK, LANE), _pad_chunks(w_in[:, c[1]:c[2]], 8, A_DQK, LANE),
        w_in[:, c[2]:c[5]], _pad_chunks(w_in[:, c[5]:c[6]], 1, B_ROPE, LANE)], axis=1).astype(BF16)
    log2e = 1.0 / math.log(2.0)
    gq = jnp.tile(jnp.pad(a_qk_g[0] * (A_DQK ** -0.5 * log2e), (0, LANE - A_DQK)), 8)[None]
    gk = jnp.tile(jnp.pad(a_qk_g[1], (0, LANE - A_DQK)), 8)[None]
    wq = _pad_chunks(b_wq_up, B_HEADS, B_DQK, 256).astype(BF16)
    gbq = jnp.tile(jnp.pad(b_qk_g[0] * (B_DQK ** -0.5 * log2e), (0, 256 - B_DQK)), B_HEADS)[None]
    gbk = jnp.pad(b_qk_g[1], (0, 256 - B_DQK))[None]
    return dict(w_in=w_in_p, gq=gq, gk=gk, gqa=b_qa_g[None], wq=wq, gbq=gbq, gkva=b_kva_g[None]), gbk


def _tile(seq, cap):
    return min(seq, cap)


def _attn_layer(xp, xs, shapes, modp, mods, g1, wp, gbk, wkv, lam_p, gsub, w_out, ctx, lam_init):
    (bp, sp), (bs, ss) = shapes
    ctx_ka, ctx_va, ctx_ckv, ctx_kr = ctx
    past = ctx_ka.shape[1]
    tabs_s = _rope_tables(ss)
    tabs_p = _rope_tables(sp)
    tmp, tms = _tile(sp, 256), _tile(ss, 512)
    qa, ka, va, qb, ckv, kr = _attn_in(xp, modp, g1, wp, tabs_p, seq=sp, per_batch_mod=False, rope=False,
                                      kv_dtype=F32, tm=tmp)
    kb, vb = _mla_expand(ckv, kr, wkv, gbk, tabs_p, seq=sp, rope=False, tm=tmp)
    r3 = lambda a, b_, s_: a.reshape(b_, s_, a.shape[-1])
    o_p = _attention(r3(qa, bp, sp), r3(qb, bp, sp), lam_p, gsub,
                     [(r3(ka, bp, sp), r3(kb, bp, sp), _values_t(r3(va, bp, sp), r3(vb, bp, sp)))],
                     lam_init=lam_init, tq=_tile(sp, 256))
    new_ak = ka.reshape(bp, sp, 2 * A_HEADS, LANE)[..., :A_DQK].reshape(bp, 1, sp, A_HEADS, 2, A_DQK)
    new_av = va.reshape(bp, 1, sp, A_HEADS, A_DV)
    new_ckv = ckv.reshape(bp, 1, sp, B_KV_LORA)
    new_kr = kr[:, :B_ROPE].reshape(bp, 1, sp, B_ROPE)
    xp1 = _proj_res(o_p.reshape(bp * sp, D), xp, modp, w_out, seq=sp, per_batch_mod=False, gate_row=2, tm=tmp)
    qa, ka, va, qb, ckv, kr = _attn_in(xs, mods, g1, wp, tabs_s, seq=ss, per_batch_mod=True, rope=True,
                                      kv_dtype=BF16, tm=tms)
    kb, vb = _mla_expand(ckv, kr, wkv, gbk, tabs_s, seq=ss, rope=True, tm=tms)
    tpast = _tile(past, 512)
    tabs_c = _rope_tables(past)
    kb_c, vb_c = _mla_expand(ctx_ckv.reshape(bs * past, B_KV_LORA),
                             jnp.pad(ctx_kr.reshape(bs * past, B_ROPE), ((0, 0), (0, LANE - B_ROPE))),
                             wkv, gbk, tabs_c, seq=past, rope=False, tm=tpast)
    ka_c = _pad_chunks(ctx_ka.reshape(bs, past, 2 * A_HEADS * A_DQK), 2 * A_HEADS, A_DQK, LANE).astype(BF16)
    va_c = ctx_va.reshape(bs, past, A_HEADS * A_DV)
    o_s = _attention(r3(qa, bs, ss), r3(qb, bs, ss), lam_p, gsub,
                     [(ka_c, r3(kb_c, bs, past), _values_t(va_c, r3(vb_c, bs, past))),
                      (r3(ka, bs, ss), r3(kb, bs, ss), _values_t(r3(va, bs, ss), r3(vb, bs, ss)))],
                     lam_init=lam_init, tq=_tile(ss, 512))
    xs1 = _proj_res(o_s.reshape(bs * ss, D), xs, mods, w_out, seq=ss, per_batch_mod=True, gate_row=2, tm=tms)
    return xp1, xs1, (new_ak, new_av, new_ckv, new_kr)


def _ssd_stream(x, mod, g1, w_in, conv_w, conv_b, bias, alog, dvec, ng, w_out, s0f, s0b, *, bsz, seq, per_batch_mod, tm):
    z, xbc, dt = _ssd_in(x, mod, g1, w_in, seq=seq, per_batch_mod=per_batch_mod, tm=tm)
    xbc = _conv_silu(xbc.reshape(bsz, seq, SSD_CONV_DIM), conv_w, conv_b, cw=512)
    dt = dt.reshape(bsz, seq, 2 * LANE)
    yf, sf = _ssd_scan(xbc, dt, bias, alog, s0f, reverse=False)
    y, sb = _ssd_scan(xbc, dt, bias, alog, s0b, (dvec, yf), reverse=True)
    x1 = _ssd_out(y.reshape(bsz * seq, SSD_DI), z, x, mod, ng, w_out, seq=seq, per_batch_mod=per_batch_mod, tm=tm)
    return x1, sf, sb


def kernel(x_prompt, x_sample, cache_a_k, cache_a_v, cache_b_ckv, cache_b_krope, state_ssd, c, c_ctx, mod_w, mod_b, norm1_g, norm2_g, attn_w_in, a_qk_g, a_lambda, a_sub_g, b_qa_g, b_wq_up, b_kva_g, b_wkv_up, b_qk_g, attn_w_out, ssd_w_in, ssd_conv_w, ssd_conv_b, ssd_dt_bias, ssd_a_log, ssd_d, ssd_norm_g, ssd_w_out, router_w, router_bias, moe_w_in, moe_w_out, shared_w_in, shared_w_out):
    bp, sp, _ = x_prompt.shape
    bs, ss, _ = x_sample.shape
    depth = mod_w.shape[0]
    shapes = ((bp, sp), (bs, ss))
    xp = x_prompt.reshape(bp * sp, D)
    xs = x_sample.reshape(bs * ss, D)

    n_c = 1 + bs
    rows = -(-n_c // 16) * 16
    c_all = jnp.pad(jnp.concatenate([c_ctx[None], c], axis=0), ((0, rows - n_c), (0, 0)))
    mod = _mod_vectors(c_all, mod_w, mod_b).reshape(depth, rows, 6, D)

    rw_t = jnp.pad(router_w.T, ((0, LANE - N_EXPERTS), (0, 0)))
    rb_col = jnp.pad(router_bias, (0, LANE - N_EXPERTS))[:, None]
    tmp, tms = _tile(sp, 256), _tile(ss, 512)

    new_caches = []
    new_ssd = []
    for l in range(depth):
        i = l // 2
        modp = mod[l, 0:1]
        mods = mod[l, 1:1 + bs]
        g1 = norm1_g[l][None]
        g2 = norm2_g[l][None]
        if l % 2 == 0:
            lam_init = 0.8 - 0.6 * math.exp(-0.3 * l)
            wp, gbk = _attn_weights(attn_w_in[i], a_qk_g[i], b_qa_g[i], b_wq_up[i], b_kva_g[i], b_qk_g[i])
            ctx = (cache_a_k[:, i], cache_a_v[:, i], cache_b_ckv[:, i], cache_b_krope[:, i])
            xp, xs, caches = _attn_layer(xp, xs, shapes, modp, mods, g1, wp, gbk, b_wkv_up[i].astype(BF16),
                                         a_lambda[i], a_sub_g[i][None], attn_w_out[i].astype(BF16), ctx, lam_init)
            new_caches.append(caches)
        else:
            w_in = ssd_w_in[i]
            n0 = SSD_DI + SSD_CONV_DIM
            w_in_p = jnp.concatenate([w_in[:, :n0], _pad_chunks(w_in[:, n0:], 2, SSD_H, LANE)], axis=1).astype(BF16)
            bias = _pad_chunks(ssd_dt_bias[i].reshape(1, 2 * SSD_H), 2, SSD_H, LANE)
            alog = _pad_chunks(ssd_a_log[i].reshape(1, 2 * SSD_H), 2, SSD_H, LANE)
            dvec = jnp.repeat(ssd_d[i], SSD_P)[None]
            args = (g1, w_in_p, ssd_conv_w[i], ssd_conv_b[i][None], bias, alog, dvec, ssd_norm_g[i][None],
                    ssd_w_out[i].astype(BF16))
            zero = jnp.zeros((bp, SSD_DI, SSD_N), F32)
            xp, sf, sb = _ssd_stream(xp, modp, *args, zero, zero, bsz=bp, seq=sp, per_batch_mod=False, tm=tmp)
            xs, _, _ = _ssd_stream(xs, mods, *args, state_ssd[:, i, 0].reshape(bs, SSD_DI, SSD_N),
                                   state_ssd[:, i, 1].reshape(bs, SSD_DI, SSD_N), bsz=bs, seq=ss,
                                   per_batch_mod=True, tm=tms)
            new_ssd.append(jnp.stack([sf, sb], axis=1).reshape(bp, 2, SSD_H, SSD_P, SSD_N))
        expert_w = (l, moe_w_in, moe_w_out, shared_w_in, shared_w_out)
        xp = _moe_block(xp, modp, g2, rw_t, rb_col, expert_w, seq=sp, per_batch_mod=False, tm=tmp)
        xs = _moe_block(xs, mods, g2, rw_t, rb_col, expert_w, seq=ss, per_batch_mod=True, tm=tms)

    cat = lambda k: jnp.concatenate([cc[k] for cc in new_caches], axis=1)
    return (xp.reshape(bp, sp, D), xs.reshape(bs, ss, D), cat(0), cat(1), cat(2), cat(3),
            jnp.stack(new_ssd, axis=1))
```

```python
import functools
import math

import jax
import jax.numpy as jnp
from jax import lax
from jax.experimental import pallas as pl
from jax.experimental.pallas import tpu as pltpu

F32 = jnp.float32
BF16 = jnp.bfloat16

D = 1024
EPS = 1e-6
GRID_W = 64
ROPE_BASE = 10000.0
LANE = 128

A_HEADS = 4
A_DQK = 64
A_DV = 128
B_HEADS = 4
B_NOPE = 128
B_ROPE = 64
B_DQK = B_NOPE + B_ROPE
B_DV = 128
B_Q_LORA = 384
B_KV_LORA = 256

SSD_DI = 2048
SSD_P = 64
SSD_H = 32
SSD_N = 128
SSD_G = 4
SSD_CHUNK = 128
SSD_CONV_DIM = SSD_DI + 2 * SSD_G * SSD_N

N_EXPERTS = 16
N_GROUPS = 4
EPG = 4
D_EXPERT = 256

VMEM_LIMIT = 56 * 1024 * 1024


def _cparams(*sem):
    return pltpu.CompilerParams(dimension_semantics=sem, vmem_limit_bytes=VMEM_LIMIT)


def _dot(a, b):
    return jnp.dot(a, b, preferred_element_type=F32)


def _dot_nt(a, b):
    return lax.dot_general(a, b, (((1,), (1,)), ((), ())), preferred_element_type=F32)


def _split3(x):
    x1 = x.astype(BF16)
    r = x - x1.astype(F32)
    x2 = r.astype(BF16)
    x3 = (r - x2.astype(F32)).astype(BF16)
    return x1, x2, x3


def _silu(x):
    return x * (1.0 / (1.0 + jnp.exp(-x)))


def _rms(x, n):
    return lax.rsqrt(jnp.sum(x * x, axis=-1, keepdims=True) * (1.0 / n) + EPS)


def _modulated(x, g, shift, scale):
    return x * _rms(x, D) * g * (1.0 + scale) + shift


def _mod_kernel(c_ref, w_ref, b_ref, o_ref):
    c = _silu(c_ref[...])
    w = w_ref[0]
    c1, c2, c3 = _split3(c)
    w1, w2, w3 = _split3(w)
    acc = _dot(c1, w1) + (_dot(c1, w2) + _dot(c2, w1)) + (_dot(c2, w2) + _dot(c1, w3) + _dot(c3, w1))
    o_ref[0] = acc + b_ref[0]


def _mod_vectors(c_all, mod_w, mod_b):
    depth = mod_w.shape[0]
    rows = c_all.shape[0]
    tn = 1536
    return pl.pallas_call(
        _mod_kernel,
        out_shape=jax.ShapeDtypeStruct((depth, rows, 6 * D), F32),
        grid=(depth, 6 * D // tn),
        in_specs=[
            pl.BlockSpec((rows, D), lambda l, j: (0, 0)),
            pl.BlockSpec((1, D, tn), lambda l, j: (l, 0, j)),
            pl.BlockSpec((1, 1, tn), lambda l, j: (l, 0, j)),
        ],
        out_specs=pl.BlockSpec((1, rows, tn), lambda l, j: (l, 0, j)),
        compiler_params=_cparams("parallel", "parallel"),
        name="mod_vectors",
    )(c_all, mod_w, mod_b.reshape(depth, 1, 6 * D))


def _rope_tables(n_tok):
    t = jnp.arange(n_tok)
    row = (t // GRID_W).astype(F32)
    col = (t % GRID_W).astype(F32)
    half = B_ROPE // 2
    inv = ROPE_BASE ** (-2.0 * jnp.arange(half // 2, dtype=F32) / half)
    ang = jnp.concatenate([row[:, None] * inv, col[:, None] * inv], axis=-1)
    cos, sin = jnp.cos(ang), jnp.sin(ang)
    zero = jnp.zeros((n_tok, LANE - 2 * half), F32)
    cos_t = jnp.concatenate([cos, cos, zero], axis=-1)
    sin_t = jnp.concatenate([-sin, sin, zero], axis=-1)
    return cos_t, sin_t


def _rope_chunk(c, cos, sin):
    lane = lax.broadcasted_iota(jnp.int32, c.shape, 1)
    swapped = jnp.where(lane < 32, pltpu.roll(c, LANE - 32, 1), pltpu.roll(c, 32, 1))
    return c * cos + swapped * sin


N_ATTN_COLS = 3328
_AQ0, _AK0, _AV0, _BQ0, _BKV0, _BKR0 = 0, 1024, 2048, 2560, 2944, 3200


def _attn_in_kernel(x_ref, mod_ref, g1_ref, w_ref, gq_ref, gk_ref, gqa_ref, wq_ref, gbq_ref, gkva_ref,
                    cos_ref, sin_ref, qa_ref, ka_ref, va_ref, qb_ref, ckv_ref, kr_ref, *, rope):
    x = x_ref[...]
    h = _modulated(x, g1_ref[...], mod_ref[0, 0:1, :], mod_ref[0, 1:2, :])
    z = _dot(h.astype(BF16), w_ref[...])
    if rope:
        cos = cos_ref[...]
        sin = sin_ref[...]
    for j in range(2 * A_HEADS):
        sl = slice(j * LANE, (j + 1) * LANE)
        q = z[:, _AQ0 + j * LANE:_AQ0 + (j + 1) * LANE]
        q = q * _rms(q, A_DQK) * gq_ref[:, sl]
        k = z[:, _AK0 + j * LANE:_AK0 + (j + 1) * LANE]
        k = k * _rms(k, A_DQK) * gk_ref[:, sl]
        if rope:
            q = _rope_chunk(q, cos, sin)
            k = _rope_chunk(k, cos, sin)
        qa_ref[:, sl] = q.astype(qa_ref.dtype)
        ka_ref[:, sl] = k.astype(ka_ref.dtype)
    va_ref[...] = z[:, _AV0:_AV0 + A_HEADS * A_DV].astype(va_ref.dtype)
    qc = z[:, _BQ0:_BQ0 + B_Q_LORA]
    qc = qc * _rms(qc, B_Q_LORA) * gqa_ref[...]
    qb = _dot(qc.astype(BF16), wq_ref[...])
    for hh in range(B_HEADS):
        lo = qb[:, hh * 256:hh * 256 + LANE]
        hi = qb[:, hh * 256 + LANE:(hh + 1) * 256]
        inv = lax.rsqrt((jnp.sum(lo * lo, axis=-1, keepdims=True)
                         + jnp.sum(hi * hi, axis=-1, keepdims=True)) * (1.0 / B_DQK) + EPS)
        lo = lo * inv * gbq_ref[:, hh * 256:hh * 256 + LANE]
        hi = hi * inv * gbq_ref[:, hh * 256 + LANE:(hh + 1) * 256]
        if rope:
            hi = _rope_chunk(hi, cos, sin)
        qb_ref[:, hh * 256:hh * 256 + LANE] = lo.astype(qb_ref.dtype)
        qb_ref[:, hh * 256 + LANE:(hh + 1) * 256] = hi.astype(qb_ref.dtype)
    kvc = z[:, _BKV0:_BKV0 + B_KV_LORA]
    ckv_ref[...] = kvc * _rms(kvc, B_KV_LORA) * gkva_ref[...]
    kr_ref[...] = z[:, _BKR0:_BKR0 + LANE]


def _attn_in(x, mod, g1, wp, rope_tabs, *, seq, per_batch_mod, rope, kv_dtype, tm):
    t = x.shape[0]
    steps_per_seq = seq // tm
    mod_map = (lambda i: (i // steps_per_seq, 0, 0)) if per_batch_mod else (lambda i: (0, 0, 0))
    tab_map = lambda i: (i % steps_per_seq, 0)
    row = lambda n: pl.BlockSpec((tm, n), lambda i: (i, 0))
    full = lambda a: pl.BlockSpec(a.shape, lambda i: (0,) * a.ndim)
    cos_t, sin_t = rope_tabs
    ins = [x, mod, g1, wp["w_in"], wp["gq"], wp["gk"], wp["gqa"], wp["wq"], wp["gbq"], wp["gkva"], cos_t, sin_t]
    in_specs = [row(D), pl.BlockSpec((1, 6, D), mod_map), full(g1), full(wp["w_in"]), full(wp["gq"]),
                full(wp["gk"]), full(wp["gqa"]), full(wp["wq"]), full(wp["gbq"]), full(wp["gkva"]),
                pl.BlockSpec((tm, LANE), tab_map), pl.BlockSpec((tm, LANE), tab_map)]
    out_shape = [
        jax.ShapeDtypeStruct((t, 1024), BF16),
        jax.ShapeDtypeStruct((t, 1024), kv_dtype),
        jax.ShapeDtypeStruct((t, 512), kv_dtype),
        jax.ShapeDtypeStruct((t, 1024), BF16),
        jax.ShapeDtypeStruct((t, B_KV_LORA), F32),
        jax.ShapeDtypeStruct((t, LANE), F32),
    ]
    out_specs = [row(1024), row(1024), row(512), row(1024), row(B_KV_LORA), row(LANE)]
    return pl.pallas_call(
        functools.partial(_attn_in_kernel, rope=rope),
        out_shape=out_shape, grid=(t // tm,), in_specs=in_specs, out_specs=out_specs,
        compiler_params=_cparams("parallel"), name="attn_in",
    )(*ins)


def _mla_expand_kernel(ckv_ref, kr_ref, w_ref, g_ref, cos_ref, sin_ref, kb_ref, vb_ref, *, rope):
    kv = _dot(ckv_ref[...].astype(BF16), w_ref[...])
    kr = kr_ref[...]
    kr_ss = jnp.sum(kr * kr, axis=-1, keepdims=True)
    for hh in range(B_HEADS):
        kn = kv[:, hh * 256:hh * 256 + LANE]
        inv = lax.rsqrt((jnp.sum(kn * kn, axis=-1, keepdims=True) + kr_ss) * (1.0 / B_DQK) + EPS)
        hi = kr * inv * g_ref[:, LANE:]
        if rope:
            hi = _rope_chunk(hi, cos_ref[...], sin_ref[...])
        kb_ref[:, hh * 256:hh * 256 + LANE] = (kn * inv * g_ref[:, :LANE]).astype(kb_ref.dtype)
        kb_ref[:, hh * 256 + LANE:(hh + 1) * 256] = hi.astype(kb_ref.dtype)
        vb_ref[:, hh * LANE:(hh + 1) * LANE] = kv[:, hh * 256 + LANE:(hh + 1) * 256].astype(vb_ref.dtype)


def _mla_expand(ckv, kr, wkv, gk, rope_tabs, *, seq, rope, tm):
    t = ckv.shape[0]
    steps_per_seq = seq // tm
    tab_map = lambda i: (i % steps_per_seq, 0)
    row = lambda n: pl.BlockSpec((tm, n), lambda i: (i, 0))
    full = lambda a: pl.BlockSpec(a.shape, lambda i: (0,) * a.ndim)
    cos_t, sin_t = rope_tabs
    return pl.pallas_call(
        functools.partial(_mla_expand_kernel, rope=rope),
        out_shape=[jax.ShapeDtypeStruct((t, 1024), BF16), jax.ShapeDtypeStruct((t, 512), BF16)],
        grid=(t // tm,),
        in_specs=[row(B_KV_LORA), row(LANE), full(wkv), full(gk),
                  pl.BlockSpec((tm, LANE), tab_map), pl.BlockSpec((tm, LANE), tab_map)],
        out_specs=[row(1024), row(512)],
        compiler_params=_cparams("parallel"), name="mla_expand",
    )(ckv, kr, wkv, gk, cos_t, sin_t)


VT_ROWS = A_DV + 16


def _values_t(va, vb):
    b, sk, _ = va.shape
    v = jnp.concatenate([va.reshape(b, sk, A_HEADS, A_DV), vb.reshape(b, sk, B_HEADS, B_DV)], axis=2)
    v = jnp.transpose(v.astype(BF16), (0, 2, 3, 1))
    return jnp.concatenate([v, jnp.ones((b, A_HEADS + B_HEADS, VT_ROWS - A_DV, sk), BF16)], axis=2)


def _attn_kernel(*refs, n_pieces, lam_init):
    qa_ref, qb_ref, lam_ref, gsub_ref = refs[:4]
    kv_refs = refs[4:4 + 3 * n_pieces]
    o_ref, s_scr, p_scr = refs[4 + 3 * n_pieces:]
    ka = kv_refs[0::3]
    kb = kv_refs[1::3]
    vt = kv_refs[2::3]
    lam_p = lam_ref[...]
    lam = (jnp.exp(jnp.sum(lam_p[0:1] * lam_p[1:2], axis=-1, keepdims=True))
           - jnp.exp(jnp.sum(lam_p[2:3] * lam_p[3:4], axis=-1, keepdims=True)) + lam_init)
    per_batch = 2 * A_HEADS + B_HEADS
    n_maps = per_batch * qa_ref.shape[0]

    def scores(n):
        bb, j = divmod(n, per_batch)
        r0 = 0
        for i in range(n_pieces):
            if j < 2 * A_HEADS:
                sl = slice(j * LANE, (j + 1) * LANE)
                s = _dot_nt(ka[i][bb, :, sl].astype(BF16), qa_ref[bb, :, sl])
            else:
                sl = slice((j - 2 * A_HEADS) * 256, (j - 2 * A_HEADS + 1) * 256)
                s = _dot_nt(kb[i][bb, :, sl], qb_ref[bb, :, sl])
            s_scr[n % 2, r0:r0 + s.shape[0], :] = s
            r0 += s.shape[0]

    def attend(n):
        bb, j = divmod(n, per_batch)
        head = j // 2 if j < 2 * A_HEADS else j - A_HEADS
        mx = jnp.max(s_scr[n % 2], axis=0, keepdims=True)
        p_scr[n % 2] = jnp.exp2(s_scr[n % 2] - mx).astype(BF16)
        acc = None
        r0 = 0
        for i in range(n_pieces):
            v = vt[i][bb, head]
            part = _dot(v, p_scr[n % 2, r0:r0 + v.shape[1], :])
            acc = part if acc is None else acc + part
            r0 += v.shape[1]
        return acc[:A_DV] / acc[A_DV:A_DV + 1]

    scores(0)
    first = None
    for n in range(n_maps):
        if n + 1 < n_maps:
            scores(n + 1)
        out = attend(n)
        bb, j = divmod(n, per_batch)
        if j >= 2 * A_HEADS:
            c0 = A_HEADS * A_DV + (j - 2 * A_HEADS) * B_DV
            o_ref[bb, :, c0:c0 + B_DV] = jnp.transpose(out).astype(o_ref.dtype)
        elif j % 2 == 0:
            first = out
        else:
            o = jnp.transpose(first - lam * out)
            o = o * _rms(o, A_DV) * gsub_ref[...] * (1.0 - lam_init)
            o_ref[bb, :, (j // 2) * A_DV:(j // 2 + 1) * A_DV] = o.astype(o_ref.dtype)


def _attention(qa, qb, lam_p, gsub, pieces, *, lam_init, tq, nb=1):
    b, s, _ = qa.shape
    qspec = pl.BlockSpec((nb, tq, 1024), lambda bi, qi: (bi, qi, 0))
    full = lambda a: pl.BlockSpec(a.shape, lambda bi, qi: (0,) * a.ndim)
    ins = [qa, qb, lam_p, gsub]
    in_specs = [qspec, qspec, full(lam_p), full(gsub)]
    kv_mode = dict(pipeline_mode=pl.Buffered(1)) if s // tq > 1 else {}
    for piece in pieces:
        for a in piece:
            ins.append(a)
            in_specs.append(pl.BlockSpec((nb,) + a.shape[1:], lambda bi, qi, nd=a.ndim: (bi,) + (0,) * (nd - 1),
                                         **kv_mode))
    sk = sum(piece[0].shape[1] for piece in pieces)
    return pl.pallas_call(
        functools.partial(_attn_kernel, n_pieces=len(pieces), lam_init=lam_init),
        out_shape=jax.ShapeDtypeStruct((b, s, 1024), BF16),
        grid=(b // nb, s // tq), in_specs=in_specs,
        out_specs=pl.BlockSpec((nb, tq, 1024), lambda bi, qi: (bi, qi, 0)),
        scratch_shapes=[pltpu.VMEM((2, sk, tq), F32), pltpu.VMEM((2, sk, tq), BF16)],
        compiler_params=_cparams("parallel", "arbitrary"), name="attn_core",
    )(*ins)


def _proj_res_kernel(o_ref, x_ref, mod_ref, w_ref, y_ref, *, gate_row):
    y_ref[...] = x_ref[...] + mod_ref[0, gate_row:gate_row + 1, :] * _dot(o_ref[...], w_ref[...])


def _proj_res(o, x, mod, w, *, seq, per_batch_mod, gate_row, tm):
    t, k = o.shape
    steps_per_seq = seq // tm
    mod_map = (lambda i: (i // steps_per_seq, 0, 0)) if per_batch_mod else (lambda i: (0, 0, 0))
    return pl.pallas_call(
        functools.partial(_proj_res_kernel, gate_row=gate_row),
        out_shape=jax.ShapeDtypeStruct((t, D), F32), grid=(t // tm,),
        in_specs=[pl.BlockSpec((tm, k), lambda i: (i, 0)), pl.BlockSpec((tm, D), lambda i: (i, 0)),
                  pl.BlockSpec((1, 6, D), mod_map), pl.BlockSpec(w.shape, lambda i: (0, 0))],
        out_specs=pl.BlockSpec((tm, D), lambda i: (i, 0)),
        compiler_params=_cparams("parallel"), name="proj_res",
    )(o, x, mod, w)


def _route(logits_t, bias_col):
    tm = logits_t.shape[1]
    scores = 1.0 / (1.0 + jnp.exp(-logits_t[:N_EXPERTS]))
    sel = scores + bias_col[:N_EXPERTS]
    grp = []
    for g in range(N_GROUPS):
        r = [sel[EPG * g + j:EPG * g + j + 1] for j in range(EPG)]
        best2 = None
        for a in range(EPG):
            for b2 in range(a + 1, EPG):
                pair = r[a] + r[b2]
                best2 = pair if best2 is None else jnp.maximum(best2, pair)
        grp.append(best2)
    best = jnp.zeros((1, tm), jnp.int32)
    best_v = grp[0]
    for g in range(1, N_GROUPS):
        better = grp[g] > best_v
        best = jnp.where(better, g, best)
        best_v = jnp.where(better, grp[g], best_v)
    v = []
    sc = []
    for j in range(EPG):
        vj = sel[j:j + 1]
        sj = scores[j:j + 1]
        for g in range(1, N_GROUPS):
            vj = jnp.where(best == g, sel[EPG * g + j:EPG * g + j + 1], vj)
            sj = jnp.where(best == g, scores[EPG * g + j:EPG * g + j + 1], sj)
        v.append(vj)
        sc.append(sj)
    picked = []
    for j in range(EPG):
        rank = jnp.zeros((1, tm), jnp.int32)
        for k in range(EPG):
            if k == j:
                continue
            ahead = (v[k] >= v[j]) if k < j else (v[k] > v[j])
            rank = rank + ahead.astype(jnp.int32)
        picked.append(rank < 2)
    den = functools.reduce(jnp.add, [jnp.where(picked[j], sc[j], 0.0) for j in range(EPG)])
    gates = [jnp.where(picked[j], sc[j] / den, 0.0) for j in range(EPG)]
    return best, gates


ROW_TILES = D // LANE


def _store_token_tiles(ref, x):
    n = x.shape[0]
    for c in range(ROW_TILES):
        ref[pl.ds(c, n, stride=ROW_TILES), :] = x[:, c * LANE:(c + 1) * LANE]


def _load_token_tiles(ref, n):
    return jnp.concatenate([ref[pl.ds(c, n, stride=ROW_TILES), :] for c in range(ROW_TILES)], axis=1)


def _moe_route_kernel(x_ref, mod_ref, g2_ref, rw_ref, rb_ref, h_ref, info_ref, carry):
    tm = x_ref.shape[0]

    @pl.when(pl.program_id(0) == 0)
    def _():
        carry[...] = jnp.zeros_like(carry)

    h = _modulated(x_ref[...], g2_ref[...], mod_ref[0, 3:4, :], mod_ref[0, 4:5, :])
    _store_token_tiles(h_ref, h)
    h1, h2, h3 = _split3(h)
    w1, w2, w3 = _split3(rw_ref[...])
    logits_t = (_dot_nt(w1, h1) + (_dot_nt(w1, h2) + _dot_nt(w2, h1))
                + (_dot_nt(w2, h2) + _dot_nt(w1, h3) + _dot_nt(w3, h1)))
    best, gates = _route(logits_t, rb_ref[...])
    grow = lax.broadcasted_iota(jnp.int32, (8, tm), 0)
    onehot = jnp.where(grow == best, 1.0, 0.0)
    ti = lax.broadcasted_iota(jnp.int32, (tm, tm), 0)
    tj = lax.broadcasted_iota(jnp.int32, (tm, tm), 1)
    upper = jnp.where(ti <= tj, 1.0, 0.0).astype(BF16)
    cum = _dot(onehot.astype(BF16), upper) + carry[:, 0:1]
    rank = jnp.sum(onehot * cum, axis=0, keepdims=True) - 1.0
    carry[...] = jnp.broadcast_to(cum[:, tm - 1:tm], carry.shape)
    info_ref[...] = jnp.concatenate(gates + [best.astype(F32), rank, jnp.zeros((2, tm), F32)], axis=0)


def _moe_route(x, mod, g2, rw_t, rb_col, *, seq, per_batch_mod, tm):
    t = x.shape[0]
    steps_per_seq = seq // tm
    mod_map = (lambda i: (i // steps_per_seq, 0, 0)) if per_batch_mod else (lambda i: (0, 0, 0))
    full = lambda a: pl.BlockSpec(a.shape, lambda i: (0,) * a.ndim)
    return pl.pallas_call(
        _moe_route_kernel,
        out_shape=[jax.ShapeDtypeStruct((t * ROW_TILES, LANE), F32), jax.ShapeDtypeStruct((8, t), F32)],
        grid=(t // tm,),
        in_specs=[pl.BlockSpec((tm, D), lambda i: (i, 0)), pl.BlockSpec((1, 6, D), mod_map), full(g2),
                  full(rw_t), full(rb_col)],
        out_specs=[pl.BlockSpec((tm * ROW_TILES, LANE), lambda i: (i, 0)), pl.BlockSpec((8, tm), lambda i: (0, i))],
        scratch_shapes=[pltpu.VMEM((8, LANE), F32)],
        compiler_params=_cparams("arbitrary"), name="moe_route",
    )(x, mod, g2, rw_t, rb_col)


def _dispatch_plan(info, t, tm):
    grp = info[4].astype(jnp.int32)
    rank = info[5].astype(jnp.int32)
    n_slots = t + N_GROUPS * tm
    n_tiles = n_slots // tm
    counts = [jnp.sum((grp == g).astype(jnp.int32)) for g in range(N_GROUPS)]
    tiles = [(c + tm - 1) // tm for c in counts]
    start_tile = [sum(tiles[:g], jnp.int32(0)) for g in range(N_GROUPS + 1)]
    dest = rank + sum(jnp.where(grp == g, start_tile[g] * tm, 0) for g in range(N_GROUPS))
    n_fill = n_slots - t
    k = jnp.arange(n_fill, dtype=jnp.int32)
    fill_key = jnp.zeros_like(k)
    seen = jnp.int32(0)
    for g in range(N_GROUPS):
        pad = tiles[g] * tm - counts[g]
        fill_key = jnp.where((k >= seen) & (k < seen + pad), start_tile[g] * tm + counts[g] + (k - seen), fill_key)
        seen = seen + pad
    fill_key = jnp.where(k >= seen, start_tile[N_GROUPS] * tm + (k - seen), fill_key)
    keys = jnp.concatenate([dest, fill_key])
    tok = jnp.concatenate([jnp.arange(t, dtype=jnp.int32), jnp.full((n_fill,), t, jnp.int32)])
    gate_rows = [jnp.concatenate([info[j], jnp.zeros((n_fill,), F32)]) for j in range(EPG)]
    _, tok_s, *gate_s = lax.sort((keys, tok, *gate_rows), num_keys=1)
    valid = tok_s < t
    slot = jnp.arange(n_slots, dtype=jnp.int32)
    dump = t + ((slot // tm) % 2) * tm + slot % tm
    src = jnp.where(valid, tok_s, 0)
    dst = jnp.where(valid, tok_s, dump)
    tile_idx = jnp.arange(n_tiles, dtype=jnp.int32)
    tile_group = sum((tile_idx >= start_tile[g]).astype(jnp.int32) for g in range(1, N_GROUPS))
    return (tile_group, src.reshape(n_tiles, 1, tm), dst.reshape(n_tiles, 1, tm), jnp.stack(gate_s, axis=1))


def _moe_sorted_kernel(tg_ref, src_ref, srcn_ref, dst_ref, gate_ref, win_ref, wout_ref, shin_ref, shout_ref,
                       h_hbm, y_hbm, hbuf, ybuf, win_bf, wout_bf, gsem, ssem, *, tm, n_tiles):
    i = pl.program_id(0)
    slot = lax.rem(i, 2)

    rt = ROW_TILES

    def token_rows(tok):
        return pl.ds(pl.multiple_of(tok * rt, rt), rt)

    def gather(idx_ref, s):
        for r in range(tm):
            pltpu.make_async_copy(h_hbm.at[token_rows(idx_ref[0, 0, r])], hbuf.at[s, pl.ds(r * rt, rt)],
                                  gsem.at[s]).start(priority=r % 2)

    def gather_wait(s):
        pltpu.make_async_copy(h_hbm.at[pl.ds(0, tm * rt)], hbuf.at[s], gsem.at[s]).wait()

    def scatter_wait(s):
        pltpu.make_async_copy(ybuf.at[s], y_hbm.at[pl.ds(0, tm * rt)], ssem.at[s]).wait()

    @pl.when(i == 0)
    def _():
        gather(src_ref, 0)

    @pl.when(i + 1 < n_tiles)
    def _():
        gather(srcn_ref, 1 - slot)

    @pl.when(i == 0)
    def _():
        win_bf[EPG] = shin_ref[...].astype(BF16)
        wout_bf[EPG * D_EXPERT:, :] = shout_ref[...].astype(BF16)

    @pl.when((i == 0) | (tg_ref[i] != tg_ref[jnp.maximum(i - 1, 0)]))
    def _():
        for e in range(EPG):
            win_bf[e] = win_ref[e].astype(BF16)
            wout_bf[e * D_EXPERT:(e + 1) * D_EXPERT, :] = wout_ref[e].astype(BF16)

    gather_wait(slot)
    x = _load_token_tiles(hbuf.at[slot], tm).astype(BF16)
    parts = []
    for e in range(EPG + 1):
        ue = _dot(x, win_bf[e])
        he = _silu(ue[:, :D_EXPERT]) * ue[:, D_EXPERT:]
        if e < EPG:
            he = he * gate_ref[:, e:e + 1]
        parts.append(he.astype(BF16))
    y = _dot(jnp.concatenate(parts, axis=1), wout_bf[...])

    @pl.when(i >= 2)
    def _():
        scatter_wait(slot)

    _store_token_tiles(ybuf.at[slot], y)
    for r in range(tm):
        pltpu.make_async_copy(ybuf.at[slot, pl.ds(r * rt, rt)], y_hbm.at[token_rows(dst_ref[0, 0, r])],
                              ssem.at[slot]).start(priority=r % 2)

    @pl.when(i == n_tiles - 1)
    def _():
        if n_tiles >= 2:
            scatter_wait(1 - slot)
        scatter_wait(slot)


def _moe_sorted(h, plan, layer, w_in, w_out, sh_in, sh_out, *, tm):
    t = h.shape[0] // ROW_TILES
    tile_group, src, dst, gates = plan
    n_tiles = src.shape[0]
    smem_blk = lambda imap: pl.BlockSpec((1, 1, tm), imap, memory_space=pltpu.SMEM)
    grid_spec = pltpu.PrefetchScalarGridSpec(
        num_scalar_prefetch=1, grid=(n_tiles,),
        in_specs=[smem_blk(lambda i, tg: (i, 0, 0)),
                  smem_blk(lambda i, tg: (jnp.minimum(i + 1, n_tiles - 1), 0, 0)),
                  smem_blk(lambda i, tg: (i, 0, 0)),
                  pl.BlockSpec((tm, EPG), lambda i, tg: (i, 0)),
                  pl.BlockSpec((None, EPG, D, 2 * D_EXPERT), lambda i, tg: (layer, tg[i], 0, 0)),
                  pl.BlockSpec((None, EPG, D_EXPERT, D), lambda i, tg: (layer, tg[i], 0, 0)),
                  pl.BlockSpec((None, D, 2 * D_EXPERT), lambda i, tg: (layer, 0, 0)),
                  pl.BlockSpec((None, D_EXPERT, D), lambda i, tg: (layer, 0, 0)),
                  pl.BlockSpec(memory_space=pl.ANY)],
        out_specs=pl.BlockSpec(memory_space=pl.ANY),
        scratch_shapes=[pltpu.VMEM((2, tm * ROW_TILES, LANE), F32), pltpu.VMEM((2, tm * ROW_TILES, LANE), F32),
                        pltpu.VMEM((EPG + 1, D, 2 * D_EXPERT), BF16), pltpu.VMEM(((EPG + 1) * D_EXPERT, D), BF16),
                        pltpu.SemaphoreType.DMA((2,)), pltpu.SemaphoreType.DMA((2,))])
    return pl.pallas_call(
        functools.partial(_moe_sorted_kernel, tm=tm, n_tiles=n_tiles),
        out_shape=jax.ShapeDtypeStruct(((t + 2 * tm) * ROW_TILES, LANE), F32), grid_spec=grid_spec,
        compiler_params=_cparams("arbitrary"), name="moe_sorted",
    )(tile_group, src, src, dst, gates, w_in, w_out, sh_in, sh_out, h)


def _moe_residual_kernel(x_ref, y_ref, mod_ref, o_ref):
    o_ref[...] = x_ref[...] + mod_ref[0, 5:6, :] * _load_token_tiles(y_ref, x_ref.shape[0])


def _moe_residual(x, y, mod, *, seq, per_batch_mod, tm):
    t = x.shape[0]
    steps_per_seq = seq // tm
    mod_map = (lambda i: (i // steps_per_seq, 0, 0)) if per_batch_mod else (lambda i: (0, 0, 0))
    row = pl.BlockSpec((tm, D), lambda i: (i, 0))
    return pl.pallas_call(
        _moe_residual_kernel, out_shape=jax.ShapeDtypeStruct((t, D), F32), grid=(t // tm,),
        in_specs=[row, pl.BlockSpec((tm * ROW_TILES, LANE), lambda i: (i, 0)), pl.BlockSpec((1, 6, D), mod_map)],
        out_specs=row,
        compiler_params=_cparams("parallel"), name="moe_residual",
    )(x, y, mod)


MOE_TILE = 256


def _moe_block(x, mod, g2, rw_t, rb_col, expert_w, *, seq, per_batch_mod, tm):
    h, info = _moe_route(x, mod, g2, rw_t, rb_col, seq=seq, per_batch_mod=per_batch_mod, tm=tm)
    plan = _dispatch_plan(info, x.shape[0], MOE_TILE)
    y = _moe_sorted(h, plan, *expert_w, tm=MOE_TILE)
    return _moe_residual(x, y, mod, seq=seq, per_batch_mod=per_batch_mod, tm=tm)


SSD_IN_COLS = SSD_DI + SSD_CONV_DIM + 2 * LANE


def _ssd_in_kernel(x_ref, mod_ref, g1_ref, w_ref, z_ref, xbc_ref, dt_ref):
    h = _modulated(x_ref[...], g1_ref[...], mod_ref[0, 0:1, :], mod_ref[0, 1:2, :])
    hb = h.astype(BF16)
    z_ref[...] = _dot(hb, w_ref[:, :SSD_DI]).astype(z_ref.dtype)
    xbc_ref[...] = _dot(hb, w_ref[:, SSD_DI:SSD_DI + SSD_CONV_DIM]).astype(xbc_ref.dtype)
    dt_ref[...] = _dot(hb, w_ref[:, SSD_DI + SSD_CONV_DIM:])


def _ssd_in(x, mod, g1, w, *, seq, per_batch_mod, tm):
    t = x.shape[0]
    steps_per_seq = seq // tm
    mod_map = (lambda i: (i // steps_per_seq, 0, 0)) if per_batch_mod else (lambda i: (0, 0, 0))
    row = lambda n: pl.BlockSpec((tm, n), lambda i: (i, 0))
    return pl.pallas_call(
        _ssd_in_kernel,
        out_shape=[jax.ShapeDtypeStruct((t, SSD_DI), BF16), jax.ShapeDtypeStruct((t, SSD_CONV_DIM), F32),
                   jax.ShapeDtypeStruct((t, 2 * LANE), F32)],
        grid=(t // tm,),
        in_specs=[row(D), pl.BlockSpec((1, 6, D), mod_map), pl.BlockSpec(g1.shape, lambda i: (0, 0)),
                  pl.BlockSpec(w.shape, lambda i: (0, 0))],
        out_specs=[row(SSD_DI), row(SSD_CONV_DIM), row(2 * LANE)],
        compiler_params=_cparams("parallel"), name="ssd_in",
    )(x, mod, g1, w)


def _conv_kernel(x_ref, w_ref, b_ref, o_ref):
    x = x_ref[0]
    s = x.shape[0]

    def taps(prev, cur, nxt):
        y = prev * w_ref[0:1, :] + cur * w_ref[1:2, :] + nxt * w_ref[2:3, :] + b_ref[...]
        return _silu(y).astype(o_ref.dtype)

    o_ref[0] = taps(pltpu.roll(x, 1, 0), x, pltpu.roll(x, s - 1, 0))
    rowi = lax.broadcasted_iota(jnp.int32, (32, x.shape[1]), 0)
    top = x[0:32]
    o_ref[0, 0:16, :] = taps(jnp.where(rowi == 0, 0.0, pltpu.roll(top, 1, 0)), top, pltpu.roll(top, 31, 0))[0:16]
    bot = x[s - 32:s]
    o_ref[0, s - 16:s, :] = taps(pltpu.roll(bot, 1, 0), bot,
                                 jnp.where(rowi == 31, 0.0, pltpu.roll(bot, 31, 0)))[16:32]


def _conv_silu(xbc, w, b, *, cw):
    bsz, s, c = xbc.shape
    return pl.pallas_call(
        _conv_kernel,
        out_shape=jax.ShapeDtypeStruct((bsz, s, c), BF16), grid=(bsz, c // cw),
        in_specs=[pl.BlockSpec((1, s, cw), lambda bi, j: (bi, 0, j)), pl.BlockSpec((3, cw), lambda bi, j: (0, j)),
                  pl.BlockSpec((1, cw), lambda bi, j: (0, j))],
        out_specs=pl.BlockSpec((1, s, cw), lambda bi, j: (bi, 0, j)),
        compiler_params=_cparams("parallel", "parallel"), name="conv_silu",
    )(xbc, w, b)


def _head_expander():
    return (jnp.arange(SSD_DI)[None, :] // SSD_P == jnp.arange(LANE)[:, None]).astype(BF16)


def _ssd_scan_kernel(*refs, reverse, add_skip):
    if add_skip:
        (x_ref, b_ref, c_ref, dt_ref, bias_ref, alog_ref, s0_ref, ex_ref, d_ref, yin_ref,
         y_ref, sfin_ref, state_t) = refs
    else:
        x_ref, b_ref, c_ref, dt_ref, bias_ref, alog_ref, s0_ref, ex_ref, y_ref, sfin_ref, state_t = refs
    ci = pl.program_id(1)
    q = SSD_CHUNK

    @pl.when(ci == 0)
    def _():
        state_t[...] = jnp.transpose(s0_ref[0])

    dtr = dt_ref[0] + bias_ref[...]
    dt = jnp.maximum(dtr, 0.0) + jnp.log(1.0 + jnp.exp(-jnp.abs(dtr)))
    a = dt * (-jnp.exp(alog_ref[...]))
    li = lax.broadcasted_iota(jnp.int32, (q, q), 0)
    si = lax.broadcasted_iota(jnp.int32, (q, q), 1)
    seen = (si >= li) if reverse else (si <= li)
    tri = jnp.where(seen, 1.0, 0.0).astype(BF16)
    a1, a2, a3 = _split3(a)
    cs = _dot(tri, a1) + _dot(tri, a2) + _dot(tri, a3)
    cs_t = jnp.transpose(cs)
    last = 0 if reverse else q - 1
    tot = cs[last:last + 1, :]
    dt_t = jnp.transpose(dt)
    dec = jnp.broadcast_to(jnp.exp(tot), (8, LANE))
    dec_hi = dec.astype(BF16)
    dec_lo = (dec - dec_hi.astype(F32)).astype(BF16)
    fac = jnp.concatenate([(dt * jnp.exp(tot - cs)).astype(BF16), jnp.exp(cs).astype(BF16), dec_hi, dec_lo], axis=0)
    fac = _dot(fac, ex_ref[...])
    dec_x = fac[2 * q:2 * q + 1, :] + fac[2 * q + 8:2 * q + 9, :]
    low = lax.broadcasted_iota(jnp.int32, (q, LANE), 1) < SSD_P
    for g in range(SSD_G):
        bg = b_ref[0, :, g * SSD_N:(g + 1) * SSD_N]
        cg = c_ref[0, :, g * SSD_N:(g + 1) * SSD_N]
        cb = _dot_nt(cg, bg)
        s_g = state_t[:, g * 512:(g + 1) * 512]
        y_off = _dot(cg, s_g.astype(BF16))
        xw_tiles = []
        for jj in range(4):
            j = g * 4 + jj
            cols = slice(j * LANE, (j + 1) * LANE)
            xb = x_ref[0, :, cols]
            ms = []
            for h in (2 * j, 2 * j + 1):
                seg = jnp.where(seen, cs[:, h:h + 1] - cs_t[h:h + 1, :], -jnp.inf)
                ms.append((cb * jnp.exp(seg) * dt_t[h:h + 1, :]).astype(BF16))
            zero = jnp.zeros_like(xb)
            rhs = jnp.concatenate([jnp.where(low, xb, zero), jnp.where(low, zero, xb)], axis=0)
            yt = y_off[:, jj * LANE:(jj + 1) * LANE] * fac[q:2 * q, cols] + _dot(jnp.concatenate(ms, axis=1), rhs)
            xt = xb.astype(F32)
            if add_skip:
                yt = yt + yin_ref[0, :, cols] + xt * d_ref[:, cols]
            y_ref[0, :, cols] = yt.astype(y_ref.dtype)
            xw_tiles.append((xt * fac[0:q, cols]).astype(BF16))
        xw = jnp.concatenate(xw_tiles, axis=1)
        bg_t = jnp.transpose(bg.astype(F32)).astype(BF16)
        state_t[:, g * 512:(g + 1) * 512] = s_g * dec_x[:, g * 512:(g + 1) * 512] + _dot(bg_t, xw)

    @pl.when(ci == pl.num_programs(1) - 1)
    def _():
        sfin_ref[0] = jnp.transpose(state_t[...])


def _ssd_scan(xbc, dt, bias, alog, s0, skip=None, *, reverse):
    bsz, s, _ = xbc.shape
    nc = s // SSD_CHUNK
    d = 1 if reverse else 0
    cmap = (lambda ci: nc - 1 - ci) if reverse else (lambda ci: ci)
    add_skip = skip is not None
    ins = [xbc, xbc, xbc, dt, bias, alog, s0, _head_expander()] + (list(skip) if add_skip else [])
    in_specs = [
        pl.BlockSpec((1, SSD_CHUNK, SSD_DI), lambda bi, ci: (bi, cmap(ci), 0)),
        pl.BlockSpec((1, SSD_CHUNK, 512), lambda bi, ci: (bi, cmap(ci), 4)),
        pl.BlockSpec((1, SSD_CHUNK, 512), lambda bi, ci: (bi, cmap(ci), 5)),
        pl.BlockSpec((1, SSD_CHUNK, LANE), lambda bi, ci: (bi, cmap(ci), d)),
        pl.BlockSpec((1, LANE), lambda bi, ci: (0, d)),
        pl.BlockSpec((1, LANE), lambda bi, ci: (0, d)),
        pl.BlockSpec((1, SSD_DI, SSD_N), lambda bi, ci: (bi, 0, 0)),
        pl.BlockSpec((LANE, SSD_DI), lambda bi, ci: (0, 0)),
    ]
    if add_skip:
        in_specs += [pl.BlockSpec((1, SSD_DI), lambda bi, ci: (0, 0)),
                     pl.BlockSpec((1, SSD_CHUNK, SSD_DI), lambda bi, ci: (bi, cmap(ci), 0))]
    return pl.pallas_call(
        functools.partial(_ssd_scan_kernel, reverse=reverse, add_skip=add_skip),
        out_shape=[jax.ShapeDtypeStruct((bsz, s, SSD_DI), F32), jax.ShapeDtypeStruct((bsz, SSD_DI, SSD_N), F32)],
        grid=(bsz, nc), in_specs=in_specs,
        out_specs=[pl.BlockSpec((1, SSD_CHUNK, SSD_DI), lambda bi, ci: (bi, cmap(ci), 0)),
                   pl.BlockSpec((1, SSD_DI, SSD_N), lambda bi, ci: (bi, 0, 0))],
        scratch_shapes=[pltpu.VMEM((SSD_N, SSD_DI), F32)],
        compiler_params=_cparams("parallel", "arbitrary"), name="ssd_scan_bwd" if reverse else "ssd_scan_fwd",
    )(*ins)


def _ssd_out_kernel(y_ref, z_ref, x_ref, mod_ref, ng_ref, w_ref, o_ref):
    y = y_ref[...] * _silu(z_ref[...].astype(F32))
    gw = SSD_DI // SSD_G
    parts = []
    for g in range(SSD_G):
        yg = y[:, g * gw:(g + 1) * gw]
        parts.append((yg * _rms(yg, gw) * ng_ref[:, g * gw:(g + 1) * gw]).astype(BF16))
    yn = jnp.concatenate(parts, axis=1)
    o_ref[...] = x_ref[...] + mod_ref[0, 2:3, :] * _dot(yn, w_ref[...])


def _ssd_out(y, z, x, mod, ng, w, *, seq, per_batch_mod, tm):
    t = x.shape[0]
    steps_per_seq = seq // tm
    mod_map = (lambda i: (i // steps_per_seq, 0, 0)) if per_batch_mod else (lambda i: (0, 0, 0))
    row = lambda n: pl.BlockSpec((tm, n), lambda i: (i, 0))
    return pl.pallas_call(
        _ssd_out_kernel,
        out_shape=jax.ShapeDtypeStruct((t, D), F32), grid=(t // tm,),
        in_specs=[row(SSD_DI), row(SSD_DI), row(D), pl.BlockSpec((1, 6, D), mod_map),
                  pl.BlockSpec(ng.shape, lambda i: (0, 0)), pl.BlockSpec(w.shape, lambda i: (0, 0))],
        out_specs=row(D),
        compiler_params=_cparams("parallel"), name="ssd_out",
    )(y, z, x, mod, ng, w)


def _pad_chunks(w, n_chunks, width, to):
    lead = w.shape[:-1]
    w = w.reshape(lead + (n_chunks, width))
    w = jnp.pad(w, [(0, 0)] * len(lead) + [(0, 0), (0, to - width)])
    return w.reshape(lead + (n_chunks * to,))


def _attn_weights(w_in, a_qk_g, b_qa_g, b_wq_up, b_kva_g, b_qk_g):
    c = [0, 512, 1024, 1536, 1920, 2176, 2240]
    w_in_p = jnp.concatenate([
        _pad_chunks(w_in[:, c[0]:c[1]], 8, A_DQK, LANE), _pad_chunks(w_in[:, c[1]:c[2]], 8, A_DQK, LANE),
        w_in[:, c[2]:c[5]], _pad_chunks(w_in[:, c[5]:c[6]], 1, B_ROPE, LANE)], axis=1).astype(BF16)
    log2e = 1.0 / math.log(2.0)
    gq = jnp.tile(jnp.pad(a_qk_g[0] * (A_DQK ** -0.5 * log2e), (0, LANE - A_DQK)), 8)[None]
    gk = jnp.tile(jnp.pad(a_qk_g[1], (0, LANE - A_DQK)), 8)[None]
    wq = _pad_chunks(b_wq_up, B_HEADS, B_DQK, 256).astype(BF16)
    gbq = jnp.tile(jnp.pad(b_qk_g[0] * (B_DQK ** -0.5 * log2e), (0, 256 - B_DQK)), B_HEADS)[None]
    gbk = jnp.pad(b_qk_g[1], (0, 256 - B_DQK))[None]
    return dict(w_in=w_in_p, gq=gq, gk=gk, gqa=b_qa_g[None], wq=wq, gbq=gbq, gkva=b_kva_g[None]), gbk


def _tile(seq, cap):
    return min(seq, cap)


def _attn_layer(xp, xs, shapes, modp, mods, g1, wp, gbk, wkv, lam_p, gsub, w_out, ctx, lam_init):
    (bp, sp), (bs, ss) = shapes
    ctx_ka, ctx_va, ctx_ckv, ctx_kr = ctx
    past = ctx_ka.shape[1]
    tabs_s = _rope_tables(ss)
    tabs_p = _rope_tables(sp)
    tmp, tms = _tile(sp, 256), _tile(ss, 512)
    qa, ka, va, qb, ckv, kr = _attn_in(xp, modp, g1, wp, tabs_p, seq=sp, per_batch_mod=False, rope=False,
                                      kv_dtype=F32, tm=tmp)
    kb, vb = _mla_expand(ckv, kr, wkv, gbk, tabs_p, seq=sp, rope=False, tm=tmp)
    r3 = lambda a, b_, s_: a.reshape(b_, s_, a.shape[-1])
    o_p = _attention(r3(qa, bp, sp), r3(qb, bp, sp), lam_p, gsub,
                     [(r3(ka, bp, sp), r3(kb, bp, sp), _values_t(r3(va, bp, sp), r3(vb, bp, sp)))],
                     lam_init=lam_init, tq=_tile(sp, 256), nb=2 if bp % 2 == 0 else 1)
    new_ak = ka.reshape(bp, sp, 2 * A_HEADS, LANE)[..., :A_DQK].reshape(bp, 1, sp, A_HEADS, 2, A_DQK)
    new_av = va.reshape(bp, 1, sp, A_HEADS, A_DV)
    new_ckv = ckv.reshape(bp, 1, sp, B_KV_LORA)
    new_kr = kr[:, :B_ROPE].reshape(bp, 1, sp, B_ROPE)
    xp1 = _proj_res(o_p.reshape(bp * sp, D), xp, modp, w_out, seq=sp, per_batch_mod=False, gate_row=2, tm=tmp)
    qa, ka, va, qb, ckv, kr = _attn_in(xs, mods, g1, wp, tabs_s, seq=ss, per_batch_mod=True, rope=True,
                                      kv_dtype=BF16, tm=tms)
    kb, vb = _mla_expand(ckv, kr, wkv, gbk, tabs_s, seq=ss, rope=True, tm=tms)
    tpast = _tile(past, 512)
    tabs_c = _rope_tables(past)
    kb_c, vb_c = _mla_expand(ctx_ckv.reshape(bs * past, B_KV_LORA),
                             jnp.pad(ctx_kr.reshape(bs * past, B_ROPE), ((0, 0), (0, LANE - B_ROPE))),
                             wkv, gbk, tabs_c, seq=past, rope=False, tm=tpast)
    ka_c = _pad_chunks(ctx_ka.reshape(bs, past, 2 * A_HEADS * A_DQK), 2 * A_HEADS, A_DQK, LANE).astype(BF16)
    va_c = ctx_va.reshape(bs, past, A_HEADS * A_DV)
    o_s = _attention(r3(qa, bs, ss), r3(qb, bs, ss), lam_p, gsub,
                     [(ka_c, r3(kb_c, bs, past), _values_t(va_c, r3(vb_c, bs, past))),
                      (r3(ka, bs, ss), r3(kb, bs, ss), _values_t(r3(va, bs, ss), r3(vb, bs, ss)))],
                     lam_init=lam_init, tq=_tile(ss, 512))
    xs1 = _proj_res(o_s.reshape(bs * ss, D), xs, mods, w_out, seq=ss, per_batch_mod=True, gate_row=2, tm=tms)
    return xp1, xs1, (new_ak, new_av, new_ckv, new_kr)


def _ssd_stream(x, mod, g1, w_in, conv_w, conv_b, bias, alog, dvec, ng, w_out, s0f, s0b, *, bsz, seq, per_batch_mod, tm):
    z, xbc, dt = _ssd_in(x, mod, g1, w_in, seq=seq, per_batch_mod=per_batch_mod, tm=tm)
    xbc = _conv_silu(xbc.reshape(bsz, seq, SSD_CONV_DIM), conv_w, conv_b, cw=512)
    dt = dt.reshape(bsz, seq, 2 * LANE)
    yf, sf = _ssd_scan(xbc, dt, bias, alog, s0f, reverse=False)
    y, sb = _ssd_scan(xbc, dt, bias, alog, s0b, (dvec, yf), reverse=True)
    x1 = _ssd_out(y.reshape(bsz * seq, SSD_DI), z, x, mod, ng, w_out, seq=seq, per_batch_mod=per_batch_mod, tm=tm)
    return x1, sf, sb


def kernel(x_prompt, x_sample, cache_a_k, cache_a_v, cache_b_ckv, cache_b_krope, state_ssd, c, c_ctx, mod_w, mod_b, norm1_g, norm2_g, attn_w_in, a_qk_g, a_lambda, a_sub_g, b_qa_g, b_wq_up, b_kva_g, b_wkv_up, b_qk_g, attn_w_out, ssd_w_in, ssd_conv_w, ssd_conv_b, ssd_dt_bias, ssd_a_log, ssd_d, ssd_norm_g, ssd_w_out, router_w, router_bias, moe_w_in, moe_w_out, shared_w_in, shared_w_out):
    bp, sp, _ = x_prompt.shape
    bs, ss, _ = x_sample.shape
    depth = mod_w.shape[0]
    shapes = ((bp, sp), (bs, ss))
    xp = x_prompt.reshape(bp * sp, D)
    xs = x_sample.reshape(bs * ss, D)

    n_c = 1 + bs
    rows = -(-n_c // 16) * 16
    c_all = jnp.pad(jnp.concatenate([c_ctx[None], c], axis=0), ((0, rows - n_c), (0, 0)))
    mod = _mod_vectors(c_all, mod_w, mod_b).reshape(depth, rows, 6, D)

    rw_t = jnp.pad(router_w.T, ((0, LANE - N_EXPERTS), (0, 0)))
    rb_col = jnp.pad(router_bias, (0, LANE - N_EXPERTS))[:, None]
    tmp, tms = _tile(sp, 256), _tile(ss, 512)

    new_caches = []
    new_ssd = []
    for l in range(depth):
        i = l // 2
        modp = mod[l, 0:1]
        mods = mod[l, 1:1 + bs]
        g1 = norm1_g[l][None]
        g2 = norm2_g[l][None]
        if l % 2 == 0:
            lam_init = 0.8 - 0.6 * math.exp(-0.3 * l)
            wp, gbk = _attn_weights(attn_w_in[i], a_qk_g[i], b_qa_g[i], b_wq_up[i], b_kva_g[i], b_qk_g[i])
            ctx = (cache_a_k[:, i], cache_a_v[:, i], cache_b_ckv[:, i], cache_b_krope[:, i])
            xp, xs, caches = _attn_layer(xp, xs, shapes, modp, mods, g1, wp, gbk, b_wkv_up[i].astype(BF16),
                                         a_lambda[i], a_sub_g[i][None], attn_w_out[i].astype(BF16), ctx, lam_init)
            new_caches.append(caches)
        else:
            w_in = ssd_w_in[i]
            n0 = SSD_DI + SSD_CONV_DIM
            w_in_p = jnp.concatenate([w_in[:, :n0], _pad_chunks(w_in[:, n0:], 2, SSD_H, LANE)], axis=1).astype(BF16)
            bias = _pad_chunks(ssd_dt_bias[i].reshape(1, 2 * SSD_H), 2, SSD_H, LANE)
            alog = _pad_chunks(ssd_a_log[i].reshape(1, 2 * SSD_H), 2, SSD_H, LANE)
            dvec = jnp.repeat(ssd_d[i], SSD_P)[None]
            args = (g1, w_in_p, ssd_conv_w[i], ssd_conv_b[i][None], bias, alog, dvec, ssd_norm_g[i][None],
                    ssd_w_out[i].astype(BF16))
            zero = jnp.zeros((bp, SSD_DI, SSD_N), F32)
            xp, sf, sb = _ssd_stream(xp, modp, *args, zero, zero, bsz=bp, seq=sp, per_batch_mod=False, tm=tmp)
            xs, _, _ = _ssd_stream(xs, mods, *args, state_ssd[:, i, 0].reshape(bs, SSD_DI, SSD_N),
                                   state_ssd[:, i, 1].reshape(bs, SSD_DI, SSD_N), bsz=bs, seq=ss,
                                   per_batch_mod=True, tm=tms)
            new_ssd.append(jnp.stack([sf, sb], axis=1).reshape(bp, 2, SSD_H, SSD_P, SSD_N))
        expert_w = (l, moe_w_in, moe_w_out, shared_w_in, shared_w_out)
        xp = _moe_block(xp, modp, g2, rw_t, rb_col, expert_w, seq=sp, per_batch_mod=False, tm=tmp)
        xs = _moe_block(xs, mods, g2, rw_t, rb_col, expert_w, seq=ss, per_batch_mod=True, tm=tms)

    cat = lambda k: jnp.concatenate([cc[k] for cc in new_caches], axis=1)
    return (xp.reshape(bp, sp, D), xs.reshape(bs, ss, D), cat(0), cat(1), cat(2), cat(3),
            jnp.stack(new_ssd, axis=1))
```

```python
import functools
import math

import jax
import jax.numpy as jnp
from jax import lax
from jax.experimental import pallas as pl
from jax.experimental.pallas import tpu as pltpu

F32 = jnp.float32
BF16 = jnp.bfloat16

D = 1024
EPS = 1e-6
GRID_W = 64
ROPE_BASE = 10000.0
LANE = 128

A_HEADS = 4
A_DQK = 64
A_DV = 128
B_HEADS = 4
B_NOPE = 128
B_ROPE = 64
B_DQK = B_NOPE + B_ROPE
B_DV = 128
B_Q_LORA = 384
B_KV_LORA = 256

SSD_DI = 2048
SSD_P = 64
SSD_H = 32
SSD_N = 128
SSD_G = 4
SSD_CHUNK = 128
SSD_CONV_DIM = SSD_DI + 2 * SSD_G * SSD_N

N_EXPERTS = 16
N_GROUPS = 4
EPG = 4
D_EXPERT = 256

VMEM_LIMIT = 56 * 1024 * 1024


def _cparams(*sem):
    return pltpu.CompilerParams(dimension_semantics=sem, vmem_limit_bytes=VMEM_LIMIT)


def _dot(a, b):
    return jnp.dot(a, b, preferred_element_type=F32)


def _dot_nt(a, b):
    return lax.dot_general(a, b, (((1,), (1,)), ((), ())), preferred_element_type=F32)


def _split3(x):
    x1 = x.astype(BF16)
    r = x - x1.astype(F32)
    x2 = r.astype(BF16)
    x3 = (r - x2.astype(F32)).astype(BF16)
    return x1, x2, x3


def _silu(x):
    return x * (1.0 / (1.0 + jnp.exp(-x)))


def _rms(x, n):
    return lax.rsqrt(jnp.sum(x * x, axis=-1, keepdims=True) * (1.0 / n) + EPS)


def _modulated(x, g, shift, scale):
    return x * _rms(x, D) * g * (1.0 + scale) + shift


def _mod_kernel(c_ref, w_ref, b_ref, o_ref):
    c = _silu(c_ref[...])
    w = w_ref[0]
    c1, c2, c3 = _split3(c)
    w1, w2, w3 = _split3(w)
    acc = _dot(c1, w1) + (_dot(c1, w2) + _dot(c2, w1)) + (_dot(c2, w2) + _dot(c1, w3) + _dot(c3, w1))
    o_ref[0] = acc + b_ref[0]


def _mod_vectors(c_all, mod_w, mod_b):
    depth = mod_w.shape[0]
    rows = c_all.shape[0]
    tn = 1536
    return pl.pallas_call(
        _mod_kernel,
        out_shape=jax.ShapeDtypeStruct((depth, rows, 6 * D), F32),
        grid=(depth, 6 * D // tn),
        in_specs=[
            pl.BlockSpec((rows, D), lambda l, j: (0, 0)),
            pl.BlockSpec((1, D, tn), lambda l, j: (l, 0, j)),
            pl.BlockSpec((1, 1, tn), lambda l, j: (l, 0, j)),
        ],
        out_specs=pl.BlockSpec((1, rows, tn), lambda l, j: (l, 0, j)),
        compiler_params=_cparams("parallel", "parallel"),
        name="mod_vectors",
    )(c_all, mod_w, mod_b.reshape(depth, 1, 6 * D))


def _rope_tables(n_tok):
    t = jnp.arange(n_tok)
    row = (t // GRID_W).astype(F32)
    col = (t % GRID_W).astype(F32)
    half = B_ROPE // 2
    inv = ROPE_BASE ** (-2.0 * jnp.arange(half // 2, dtype=F32) / half)
    ang = jnp.concatenate([row[:, None] * inv, col[:, None] * inv], axis=-1)
    cos, sin = jnp.cos(ang), jnp.sin(ang)
    zero = jnp.zeros((n_tok, LANE - 2 * half), F32)
    cos_t = jnp.concatenate([cos, cos, zero], axis=-1)
    sin_t = jnp.concatenate([-sin, sin, zero], axis=-1)
    return cos_t, sin_t


def _rope_chunk(c, cos, sin):
    lane = lax.broadcasted_iota(jnp.int32, c.shape, 1)
    swapped = jnp.where(lane < 32, pltpu.roll(c, LANE - 32, 1), pltpu.roll(c, 32, 1))
    return c * cos + swapped * sin


N_ATTN_COLS = 3328
_AQ0, _AK0, _AV0, _BQ0, _BKV0, _BKR0 = 0, 1024, 2048, 2560, 2944, 3200


def _attn_in_kernel(x_ref, mod_ref, g1_ref, w_ref, gq_ref, gk_ref, gqa_ref, wq_ref, gbq_ref, gkva_ref,
                    cos_ref, sin_ref, qa_ref, ka_ref, va_ref, qb_ref, ckv_ref, kr_ref, *, rope):
    x = x_ref[...]
    h = _modulated(x, g1_ref[...], mod_ref[0, 0:1, :], mod_ref[0, 1:2, :])
    z = _dot(h.astype(BF16), w_ref[...])
    if rope:
        cos = cos_ref[...]
        sin = sin_ref[...]
    for j in range(2 * A_HEADS):
        sl = slice(j * LANE, (j + 1) * LANE)
        q = z[:, _AQ0 + j * LANE:_AQ0 + (j + 1) * LANE]
        q = q * _rms(q, A_DQK) * gq_ref[:, sl]
        k = z[:, _AK0 + j * LANE:_AK0 + (j + 1) * LANE]
        k = k * _rms(k, A_DQK) * gk_ref[:, sl]
        if rope:
            q = _rope_chunk(q, cos, sin)
            k = _rope_chunk(k, cos, sin)
        qa_ref[:, sl] = q.astype(qa_ref.dtype)
        ka_ref[:, sl] = k.astype(ka_ref.dtype)
    va_ref[...] = z[:, _AV0:_AV0 + A_HEADS * A_DV].astype(va_ref.dtype)
    qc = z[:, _BQ0:_BQ0 + B_Q_LORA]
    qc = qc * _rms(qc, B_Q_LORA) * gqa_ref[...]
    qb = _dot(qc.astype(BF16), wq_ref[...])
    for hh in range(B_HEADS):
        lo = qb[:, hh * 256:hh * 256 + LANE]
        hi = qb[:, hh * 256 + LANE:(hh + 1) * 256]
        inv = lax.rsqrt((jnp.sum(lo * lo, axis=-1, keepdims=True)
                         + jnp.sum(hi * hi, axis=-1, keepdims=True)) * (1.0 / B_DQK) + EPS)
        lo = lo * inv * gbq_ref[:, hh * 256:hh * 256 + LANE]
        hi = hi * inv * gbq_ref[:, hh * 256 + LANE:(hh + 1) * 256]
        if rope:
            hi = _rope_chunk(hi, cos, sin)
        qb_ref[:, hh * 256:hh * 256 + LANE] = lo.astype(qb_ref.dtype)
        qb_ref[:, hh * 256 + LANE:(hh + 1) * 256] = hi.astype(qb_ref.dtype)
    kvc = z[:, _BKV0:_BKV0 + B_KV_LORA]
    ckv_ref[...] = kvc * _rms(kvc, B_KV_LORA) * gkva_ref[...]
    kr_ref[...] = z[:, _BKR0:_BKR0 + LANE]


def _attn_in(x, mod, g1, wp, rope_tabs, *, seq, per_batch_mod, rope, kv_dtype, tm):
    t = x.shape[0]
    steps_per_seq = seq // tm
    mod_map = (lambda i: (i // steps_per_seq, 0, 0)) if per_batch_mod else (lambda i: (0, 0, 0))
    tab_map = lambda i: (i % steps_per_seq, 0)
    row = lambda n: pl.BlockSpec((tm, n), lambda i: (i, 0))
    full = lambda a: pl.BlockSpec(a.shape, lambda i: (0,) * a.ndim)
    cos_t, sin_t = rope_tabs
    ins = [x, mod, g1, wp["w_in"], wp["gq"], wp["gk"], wp["gqa"], wp["wq"], wp["gbq"], wp["gkva"], cos_t, sin_t]
    in_specs = [row(D), pl.BlockSpec((1, 6, D), mod_map), full(g1), full(wp["w_in"]), full(wp["gq"]),
                full(wp["gk"]), full(wp["gqa"]), full(wp["wq"]), full(wp["gbq"]), full(wp["gkva"]),
                pl.BlockSpec((tm, LANE), tab_map), pl.BlockSpec((tm, LANE), tab_map)]
    out_shape = [
        jax.ShapeDtypeStruct((t, 1024), BF16),
        jax.ShapeDtypeStruct((t, 1024), kv_dtype),
        jax.ShapeDtypeStruct((t, 512), kv_dtype),
        jax.ShapeDtypeStruct((t, 1024), BF16),
        jax.ShapeDtypeStruct((t, B_KV_LORA), F32),
        jax.ShapeDtypeStruct((t, LANE), F32),
    ]
    out_specs = [row(1024), row(1024), row(512), row(1024), row(B_KV_LORA), row(LANE)]
    return pl.pallas_call(
        functools.partial(_attn_in_kernel, rope=rope),
        out_shape=out_shape, grid=(t // tm,), in_specs=in_specs, out_specs=out_specs,
        compiler_params=_cparams("parallel"), name="attn_in",
    )(*ins)


def _mla_expand_kernel(ckv_ref, kr_ref, w_ref, g_ref, cos_ref, sin_ref, kb_ref, vb_ref, *, rope):
    kv = _dot(ckv_ref[...].astype(BF16), w_ref[...])
    kr = kr_ref[...]
    kr_ss = jnp.sum(kr * kr, axis=-1, keepdims=True)
    for hh in range(B_HEADS):
        kn = kv[:, hh * 256:hh * 256 + LANE]
        inv = lax.rsqrt((jnp.sum(kn * kn, axis=-1, keepdims=True) + kr_ss) * (1.0 / B_DQK) + EPS)
        hi = kr * inv * g_ref[:, LANE:]
        if rope:
            hi = _rope_chunk(hi, cos_ref[...], sin_ref[...])
        kb_ref[:, hh * 256:hh * 256 + LANE] = (kn * inv * g_ref[:, :LANE]).astype(kb_ref.dtype)
        kb_ref[:, hh * 256 + LANE:(hh + 1) * 256] = hi.astype(kb_ref.dtype)
        vb_ref[:, hh * LANE:(hh + 1) * LANE] = kv[:, hh * 256 + LANE:(hh + 1) * 256].astype(vb_ref.dtype)


def _mla_expand(ckv, kr, wkv, gk, rope_tabs, *, seq, rope, tm):
    t = ckv.shape[0]
    steps_per_seq = seq // tm
    tab_map = lambda i: (i % steps_per_seq, 0)
    row = lambda n: pl.BlockSpec((tm, n), lambda i: (i, 0))
    full = lambda a: pl.BlockSpec(a.shape, lambda i: (0,) * a.ndim)
    cos_t, sin_t = rope_tabs
    return pl.pallas_call(
        functools.partial(_mla_expand_kernel, rope=rope),
        out_shape=[jax.ShapeDtypeStruct((t, 1024), BF16), jax.ShapeDtypeStruct((t, 512), BF16)],
        grid=(t // tm,),
        in_specs=[row(B_KV_LORA), row(LANE), full(wkv), full(gk),
                  pl.BlockSpec((tm, LANE), tab_map), pl.BlockSpec((tm, LANE), tab_map)],
        out_specs=[row(1024), row(512)],
        compiler_params=_cparams("parallel"), name="mla_expand",
    )(ckv, kr, wkv, gk, cos_t, sin_t)


VT_ROWS = A_DV + 16


def _values_t(va, vb):
    b, sk, _ = va.shape
    v = jnp.concatenate([va.reshape(b, sk, A_HEADS, A_DV), vb.reshape(b, sk, B_HEADS, B_DV)], axis=2)
    v = jnp.transpose(v.astype(BF16), (0, 2, 3, 1))
    return jnp.concatenate([v, jnp.ones((b, A_HEADS + B_HEADS, VT_ROWS - A_DV, sk), BF16)], axis=2)


def _attn_kernel(*refs, n_pieces, lam_init):
    qa_ref, qb_ref, lam_ref, gsub_ref = refs[:4]
    kv_refs = refs[4:4 + 3 * n_pieces]
    o_ref, s_scr, p_scr = refs[4 + 3 * n_pieces:]
    ka = kv_refs[0::3]
    kb = kv_refs[1::3]
    vt = kv_refs[2::3]
    lam_p = lam_ref[...]
    lam = (jnp.exp(jnp.sum(lam_p[0:1] * lam_p[1:2], axis=-1, keepdims=True))
           - jnp.exp(jnp.sum(lam_p[2:3] * lam_p[3:4], axis=-1, keepdims=True)) + lam_init)
    per_batch = 2 * A_HEADS + B_HEADS
    n_maps = per_batch * qa_ref.shape[0]

    def scores(n):
        bb, j = divmod(n, per_batch)
        r0 = 0
        for i in range(n_pieces):
            if j < 2 * A_HEADS:
                sl = slice(j * LANE, (j + 1) * LANE)
                s = _dot_nt(ka[i][bb, :, sl].astype(BF16), qa_ref[bb, :, sl])
            else:
                sl = slice((j - 2 * A_HEADS) * 256, (j - 2 * A_HEADS + 1) * 256)
                s = _dot_nt(kb[i][bb, :, sl], qb_ref[bb, :, sl])
            s_scr[n % 2, r0:r0 + s.shape[0], :] = s
            r0 += s.shape[0]

    def attend(n):
        bb, j = divmod(n, per_batch)
        head = j // 2 if j < 2 * A_HEADS else j - A_HEADS
        mx = jnp.max(s_scr[n % 2], axis=0, keepdims=True)
        p_scr[n % 2] = jnp.exp2(s_scr[n % 2] - mx).astype(BF16)
        acc = None
        r0 = 0
        for i in range(n_pieces):
            v = vt[i][bb, head]
            part = _dot(v, p_scr[n % 2, r0:r0 + v.shape[1], :])
            acc = part if acc is None else acc + part
            r0 += v.shape[1]
        return acc[:A_DV] / acc[A_DV:A_DV + 1]

    scores(0)
    first = None
    for n in range(n_maps):
        if n + 1 < n_maps:
            scores(n + 1)
        out = attend(n)
        bb, j = divmod(n, per_batch)
        if j >= 2 * A_HEADS:
            c0 = A_HEADS * A_DV + (j - 2 * A_HEADS) * B_DV
            o_ref[bb, :, c0:c0 + B_DV] = jnp.transpose(out).astype(o_ref.dtype)
        elif j % 2 == 0:
            first = out
        else:
            o = jnp.transpose(first - lam * out)
            o = o * _rms(o, A_DV) * gsub_ref[...] * (1.0 - lam_init)
            o_ref[bb, :, (j // 2) * A_DV:(j // 2 + 1) * A_DV] = o.astype(o_ref.dtype)


def _attention(qa, qb, lam_p, gsub, pieces, *, lam_init, tq, nb=1):
    b, s, _ = qa.shape
    qspec = pl.BlockSpec((nb, tq, 1024), lambda bi, qi: (bi, qi, 0))
    full = lambda a: pl.BlockSpec(a.shape, lambda bi, qi: (0,) * a.ndim)
    ins = [qa, qb, lam_p, gsub]
    in_specs = [qspec, qspec, full(lam_p), full(gsub)]
    kv_mode = dict(pipeline_mode=pl.Buffered(1)) if s // tq > 1 else {}
    for piece in pieces:
        for a in piece:
            ins.append(a)
            in_specs.append(pl.BlockSpec((nb,) + a.shape[1:], lambda bi, qi, nd=a.ndim: (bi,) + (0,) * (nd - 1),
                                         **kv_mode))
    sk = sum(piece[0].shape[1] for piece in pieces)
    return pl.pallas_call(
        functools.partial(_attn_kernel, n_pieces=len(pieces), lam_init=lam_init),
        out_shape=jax.ShapeDtypeStruct((b, s, 1024), BF16),
        grid=(b // nb, s // tq), in_specs=in_specs,
        out_specs=pl.BlockSpec((nb, tq, 1024), lambda bi, qi: (bi, qi, 0)),
        scratch_shapes=[pltpu.VMEM((2, sk, tq), F32), pltpu.VMEM((2, sk, tq), BF16)],
        compiler_params=_cparams("parallel", "arbitrary"), name="attn_core",
    )(*ins)


def _proj_res_kernel(o_ref, x_ref, mod_ref, w_ref, y_ref, *, gate_row):
    y_ref[...] = x_ref[...] + mod_ref[0, gate_row:gate_row + 1, :] * _dot(o_ref[...], w_ref[...])


def _proj_res(o, x, mod, w, *, seq, per_batch_mod, gate_row, tm):
    t, k = o.shape
    steps_per_seq = seq // tm
    mod_map = (lambda i: (i // steps_per_seq, 0, 0)) if per_batch_mod else (lambda i: (0, 0, 0))
    return pl.pallas_call(
        functools.partial(_proj_res_kernel, gate_row=gate_row),
        out_shape=jax.ShapeDtypeStruct((t, D), F32), grid=(t // tm,),
        in_specs=[pl.BlockSpec((tm, k), lambda i: (i, 0)), pl.BlockSpec((tm, D), lambda i: (i, 0)),
                  pl.BlockSpec((1, 6, D), mod_map), pl.BlockSpec(w.shape, lambda i: (0, 0))],
        out_specs=pl.BlockSpec((tm, D), lambda i: (i, 0)),
        compiler_params=_cparams("parallel"), name="proj_res",
    )(o, x, mod, w)


def _route(logits_t, bias_col):
    tm = logits_t.shape[1]
    scores = 1.0 / (1.0 + jnp.exp(-logits_t[:N_EXPERTS]))
    sel = scores + bias_col[:N_EXPERTS]
    grp = []
    for g in range(N_GROUPS):
        r = [sel[EPG * g + j:EPG * g + j + 1] for j in range(EPG)]
        best2 = None
        for a in range(EPG):
            for b2 in range(a + 1, EPG):
                pair = r[a] + r[b2]
                best2 = pair if best2 is None else jnp.maximum(best2, pair)
        grp.append(best2)
    best = jnp.zeros((1, tm), jnp.int32)
    best_v = grp[0]
    for g in range(1, N_GROUPS):
        better = grp[g] > best_v
        best = jnp.where(better, g, best)
        best_v = jnp.where(better, grp[g], best_v)
    v = []
    sc = []
    for j in range(EPG):
        vj = sel[j:j + 1]
        sj = scores[j:j + 1]
        for g in range(1, N_GROUPS):
            vj = jnp.where(best == g, sel[EPG * g + j:EPG * g + j + 1], vj)
            sj = jnp.where(best == g, scores[EPG * g + j:EPG * g + j + 1], sj)
        v.append(vj)
        sc.append(sj)
    picked = []
    for j in range(EPG):
        rank = jnp.zeros((1, tm), jnp.int32)
        for k in range(EPG):
            if k == j:
                continue
            ahead = (v[k] >= v[j]) if k < j else (v[k] > v[j])
            rank = rank + ahead.astype(jnp.int32)
        picked.append(rank < 2)
    den = functools.reduce(jnp.add, [jnp.where(picked[j], sc[j], 0.0) for j in range(EPG)])
    gates = [jnp.where(picked[j], sc[j] / den, 0.0) for j in range(EPG)]
    return best, gates


ROW_TILES = D // LANE


def _store_token_tiles(ref, x):
    n = x.shape[0]
    for c in range(ROW_TILES):
        ref[pl.ds(c, n, stride=ROW_TILES), :] = x[:, c * LANE:(c + 1) * LANE]


def _load_token_tiles(ref, n):
    return jnp.concatenate([ref[pl.ds(c, n, stride=ROW_TILES), :] for c in range(ROW_TILES)], axis=1)


def _moe_route_kernel(x_ref, mod_ref, g2_ref, rw_ref, rb_ref, h_ref, info_ref, carry):
    tm = x_ref.shape[0]

    @pl.when(pl.program_id(0) == 0)
    def _():
        carry[...] = jnp.zeros_like(carry)

    h = _modulated(x_ref[...], g2_ref[...], mod_ref[0, 3:4, :], mod_ref[0, 4:5, :])
    _store_token_tiles(h_ref, h)
    h1, h2, h3 = _split3(h)
    w1, w2, w3 = _split3(rw_ref[...])
    logits_t = (_dot_nt(w1, h1) + (_dot_nt(w1, h2) + _dot_nt(w2, h1))
                + (_dot_nt(w2, h2) + _dot_nt(w1, h3) + _dot_nt(w3, h1)))
    best, gates = _route(logits_t, rb_ref[...])
    grow = lax.broadcasted_iota(jnp.int32, (8, tm), 0)
    onehot = jnp.where(grow == best, 1.0, 0.0)
    ti = lax.broadcasted_iota(jnp.int32, (tm, tm), 0)
    tj = lax.broadcasted_iota(jnp.int32, (tm, tm), 1)
    upper = jnp.where(ti <= tj, 1.0, 0.0).astype(BF16)
    cum = _dot(onehot.astype(BF16), upper) + carry[:, 0:1]
    rank = jnp.sum(onehot * cum, axis=0, keepdims=True) - 1.0
    carry[...] = jnp.broadcast_to(cum[:, tm - 1:tm], carry.shape)
    info_ref[...] = jnp.concatenate(gates + [best.astype(F32), rank, jnp.zeros((2, tm), F32)], axis=0)


def _moe_route(x, mod, g2, rw_t, rb_col, *, seq, per_batch_mod, tm):
    t = x.shape[0]
    steps_per_seq = seq // tm
    mod_map = (lambda i: (i // steps_per_seq, 0, 0)) if per_batch_mod else (lambda i: (0, 0, 0))
    full = lambda a: pl.BlockSpec(a.shape, lambda i: (0,) * a.ndim)
    return pl.pallas_call(
        _moe_route_kernel,
        out_shape=[jax.ShapeDtypeStruct((t * ROW_TILES, LANE), F32), jax.ShapeDtypeStruct((8, t), F32)],
        grid=(t // tm,),
        in_specs=[pl.BlockSpec((tm, D), lambda i: (i, 0)), pl.BlockSpec((1, 6, D), mod_map), full(g2),
                  full(rw_t), full(rb_col)],
        out_specs=[pl.BlockSpec((tm * ROW_TILES, LANE), lambda i: (i, 0)), pl.BlockSpec((8, tm), lambda i: (0, i))],
        scratch_shapes=[pltpu.VMEM((8, LANE), F32)],
        compiler_params=_cparams("arbitrary"), name="moe_route",
    )(x, mod, g2, rw_t, rb_col)


def _dispatch_plan(info, t, tm):
    grp = info[4].astype(jnp.int32)
    rank = info[5].astype(jnp.int32)
    n_slots = t + N_GROUPS * tm
    n_tiles = n_slots // tm
    counts = [jnp.sum((grp == g).astype(jnp.int32)) for g in range(N_GROUPS)]
    tiles = [(c + tm - 1) // tm for c in counts]
    start_tile = [sum(tiles[:g], jnp.int32(0)) for g in range(N_GROUPS + 1)]
    dest = rank + sum(jnp.where(grp == g, start_tile[g] * tm, 0) for g in range(N_GROUPS))
    n_fill = n_slots - t
    k = jnp.arange(n_fill, dtype=jnp.int32)
    fill_key = jnp.zeros_like(k)
    seen = jnp.int32(0)
    for g in range(N_GROUPS):
        pad = tiles[g] * tm - counts[g]
        fill_key = jnp.where((k >= seen) & (k < seen + pad), start_tile[g] * tm + counts[g] + (k - seen), fill_key)
        seen = seen + pad
    fill_key = jnp.where(k >= seen, start_tile[N_GROUPS] * tm + (k - seen), fill_key)
    keys = jnp.concatenate([dest, fill_key])
    tok = jnp.concatenate([jnp.arange(t, dtype=jnp.int32), jnp.full((n_fill,), t, jnp.int32)])
    gate_rows = [jnp.concatenate([info[j], jnp.zeros((n_fill,), F32)]) for j in range(EPG)]
    _, tok_s, *gate_s = lax.sort((keys, tok, *gate_rows), num_keys=1)
    valid = tok_s < t
    slot = jnp.arange(n_slots, dtype=jnp.int32)
    dump = t + ((slot // tm) % 2) * tm + slot % tm
    src = jnp.where(valid, tok_s, 0)
    dst = jnp.where(valid, tok_s, dump)
    tile_idx = jnp.arange(n_tiles, dtype=jnp.int32)
    tile_group = sum((tile_idx >= start_tile[g]).astype(jnp.int32) for g in range(1, N_GROUPS))
    return (tile_group, src.reshape(n_tiles, 1, tm), dst.reshape(n_tiles, 1, tm), jnp.stack(gate_s, axis=1))


def _moe_sorted_kernel(tg_ref, src_ref, srcn_ref, dst_ref, gate_ref, win_ref, wout_ref, shin_ref, shout_ref,
                       h_hbm, y_hbm, hbuf, ybuf, win_bf, wout_bf, gsem, ssem, *, tm, n_tiles):
    i = pl.program_id(0)
    slot = lax.rem(i, 2)

    rt = ROW_TILES

    def token_rows(tok):
        return pl.ds(pl.multiple_of(tok * rt, rt), rt)

    def gather(idx_ref, s):
        for r in range(tm):
            pltpu.make_async_copy(h_hbm.at[token_rows(idx_ref[0, 0, r])], hbuf.at[s, pl.ds(r * rt, rt)],
                                  gsem.at[s]).start(priority=r % 2)

    def gather_wait(s):
        pltpu.make_async_copy(h_hbm.at[pl.ds(0, tm * rt)], hbuf.at[s], gsem.at[s]).wait()

    def scatter_wait(s):
        pltpu.make_async_copy(ybuf.at[s], y_hbm.at[pl.ds(0, tm * rt)], ssem.at[s]).wait()

    @pl.when(i == 0)
    def _():
        gather(src_ref, 0)

    @pl.when(i + 1 < n_tiles)
    def _():
        gather(srcn_ref, 1 - slot)

    @pl.when(i == 0)
    def _():
        win_bf[EPG] = shin_ref[...].astype(BF16)
        wout_bf[EPG * D_EXPERT:, :] = shout_ref[...].astype(BF16)

    @pl.when((i == 0) | (tg_ref[i] != tg_ref[jnp.maximum(i - 1, 0)]))
    def _():
        for e in range(EPG):
            win_bf[e] = win_ref[e].astype(BF16)
            wout_bf[e * D_EXPERT:(e + 1) * D_EXPERT, :] = wout_ref[e].astype(BF16)

    gather_wait(slot)
    x = _load_token_tiles(hbuf.at[slot], tm).astype(BF16)
    parts = []
    for e in range(EPG + 1):
        ue = _dot(x, win_bf[e])
        he = _silu(ue[:, :D_EXPERT]) * ue[:, D_EXPERT:]
        if e < EPG:
            he = he * gate_ref[:, e:e + 1]
        parts.append(he.astype(BF16))
    y = _dot(jnp.concatenate(parts, axis=1), wout_bf[...])

    @pl.when(i >= 2)
    def _():
        scatter_wait(slot)

    _store_token_tiles(ybuf.at[slot], y)
    for r in range(tm):
        pltpu.make_async_copy(ybuf.at[slot, pl.ds(r * rt, rt)], y_hbm.at[token_rows(dst_ref[0, 0, r])],
                              ssem.at[slot]).start(priority=r % 2)

    @pl.when(i == n_tiles - 1)
    def _():
        if n_tiles >= 2:
            scatter_wait(1 - slot)
        scatter_wait(slot)


def _moe_sorted(h, plan, layer, w_in, w_out, sh_in, sh_out, *, tm):
    t = h.shape[0] // ROW_TILES
    tile_group, src, dst, gates = plan
    n_tiles = src.shape[0]
    smem_blk = lambda imap: pl.BlockSpec((1, 1, tm), imap, memory_space=pltpu.SMEM)
    grid_spec = pltpu.PrefetchScalarGridSpec(
        num_scalar_prefetch=1, grid=(n_tiles,),
        in_specs=[smem_blk(lambda i, tg: (i, 0, 0)),
                  smem_blk(lambda i, tg: (jnp.minimum(i + 1, n_tiles - 1), 0, 0)),
                  smem_blk(lambda i, tg: (i, 0, 0)),
                  pl.BlockSpec((tm, EPG), lambda i, tg: (i, 0)),
                  pl.BlockSpec((None, EPG, D, 2 * D_EXPERT), lambda i, tg: (layer, tg[i], 0, 0)),
                  pl.BlockSpec((None, EPG, D_EXPERT, D), lambda i, tg: (layer, tg[i], 0, 0)),
                  pl.BlockSpec((None, D, 2 * D_EXPERT), lambda i, tg: (layer, 0, 0)),
                  pl.BlockSpec((None, D_EXPERT, D), lambda i, tg: (layer, 0, 0)),
                  pl.BlockSpec(memory_space=pl.ANY)],
        out_specs=pl.BlockSpec(memory_space=pl.ANY),
        scratch_shapes=[pltpu.VMEM((2, tm * ROW_TILES, LANE), F32), pltpu.VMEM((2, tm * ROW_TILES, LANE), F32),
                        pltpu.VMEM((EPG + 1, D, 2 * D_EXPERT), BF16), pltpu.VMEM(((EPG + 1) * D_EXPERT, D), BF16),
                        pltpu.SemaphoreType.DMA((2,)), pltpu.SemaphoreType.DMA((2,))])
    return pl.pallas_call(
        functools.partial(_moe_sorted_kernel, tm=tm, n_tiles=n_tiles),
        out_shape=jax.ShapeDtypeStruct(((t + 2 * tm) * ROW_TILES, LANE), F32), grid_spec=grid_spec,
        compiler_params=_cparams("arbitrary"), name="moe_sorted",
    )(tile_group, src, src, dst, gates, w_in, w_out, sh_in, sh_out, h)


def _moe_residual_kernel(x_ref, y_ref, mod_ref, o_ref):
    o_ref[...] = x_ref[...] + mod_ref[0, 5:6, :] * _load_token_tiles(y_ref, x_ref.shape[0])


def _moe_residual(x, y, mod, *, seq, per_batch_mod, tm):
    t = x.shape[0]
    steps_per_seq = seq // tm
    mod_map = (lambda i: (i // steps_per_seq, 0, 0)) if per_batch_mod else (lambda i: (0, 0, 0))
    row = pl.BlockSpec((tm, D), lambda i: (i, 0))
    return pl.pallas_call(
        _moe_residual_kernel, out_shape=jax.ShapeDtypeStruct((t, D), F32), grid=(t // tm,),
        in_specs=[row, pl.BlockSpec((tm * ROW_TILES, LANE), lambda i: (i, 0)), pl.BlockSpec((1, 6, D), mod_map)],
        out_specs=row,
        compiler_params=_cparams("parallel"), name="moe_residual",
    )(x, y, mod)


MOE_TILE = 256


def _moe_block(x, mod, g2, rw_t, rb_col, expert_w, *, seq, per_batch_mod, tm):
    h, info = _moe_route(x, mod, g2, rw_t, rb_col, seq=seq, per_batch_mod=per_batch_mod, tm=tm)
    plan = _dispatch_plan(info, x.shape[0], MOE_TILE)
    y = _moe_sorted(h, plan, *expert_w, tm=MOE_TILE)
    return _moe_residual(x, y, mod, seq=seq, per_batch_mod=per_batch_mod, tm=tm)


SSD_IN_COLS = SSD_DI + SSD_CONV_DIM + 2 * LANE


def _ssd_in_kernel(x_ref, mod_ref, g1_ref, w_ref, z_ref, xbc_ref, dt_ref):
    h = _modulated(x_ref[...], g1_ref[...], mod_ref[0, 0:1, :], mod_ref[0, 1:2, :])
    hb = h.astype(BF16)
    z_ref[...] = _dot(hb, w_ref[:, :SSD_DI]).astype(z_ref.dtype)
    xbc_ref[...] = _dot(hb, w_ref[:, SSD_DI:SSD_DI + SSD_CONV_DIM]).astype(xbc_ref.dtype)
    dt_ref[...] = _dot(hb, w_ref[:, SSD_DI + SSD_CONV_DIM:])


def _ssd_in(x, mod, g1, w, *, seq, per_batch_mod, tm):
    t = x.shape[0]
    steps_per_seq = seq // tm
    mod_map = (lambda i: (i // steps_per_seq, 0, 0)) if per_batch_mod else (lambda i: (0, 0, 0))
    row = lambda n: pl.BlockSpec((tm, n), lambda i: (i, 0))
    return pl.pallas_call(
        _ssd_in_kernel,
        out_shape=[jax.ShapeDtypeStruct((t, SSD_DI), BF16), jax.ShapeDtypeStruct((t, SSD_CONV_DIM), F32),
                   jax.ShapeDtypeStruct((t, 2 * LANE), F32)],
        grid=(t // tm,),
        in_specs=[row(D), pl.BlockSpec((1, 6, D), mod_map), pl.BlockSpec(g1.shape, lambda i: (0, 0)),
                  pl.BlockSpec(w.shape, lambda i: (0, 0))],
        out_specs=[row(SSD_DI), row(SSD_CONV_DIM), row(2 * LANE)],
        compiler_params=_cparams("parallel"), name="ssd_in",
    )(x, mod, g1, w)


def _conv_kernel(x_ref, w_ref, b_ref, o_ref):
    x = x_ref[0]
    s = x.shape[0]

    def taps(prev, cur, nxt):
        y = prev * w_ref[0:1, :] + cur * w_ref[1:2, :] + nxt * w_ref[2:3, :] + b_ref[...]
        return _silu(y).astype(o_ref.dtype)

    o_ref[0] = taps(pltpu.roll(x, 1, 0), x, pltpu.roll(x, s - 1, 0))
    rowi = lax.broadcasted_iota(jnp.int32, (32, x.shape[1]), 0)
    top = x[0:32]
    o_ref[0, 0:16, :] = taps(jnp.where(rowi == 0, 0.0, pltpu.roll(top, 1, 0)), top, pltpu.roll(top, 31, 0))[0:16]
    bot = x[s - 32:s]
    o_ref[0, s - 16:s, :] = taps(pltpu.roll(bot, 1, 0), bot,
                                 jnp.where(rowi == 31, 0.0, pltpu.roll(bot, 31, 0)))[16:32]


def _conv_silu(xbc, w, b, *, cw):
    bsz, s, c = xbc.shape
    return pl.pallas_call(
        _conv_kernel,
        out_shape=jax.ShapeDtypeStruct((bsz, s, c), BF16), grid=(bsz, c // cw),
        in_specs=[pl.BlockSpec((1, s, cw), lambda bi, j: (bi, 0, j)), pl.BlockSpec((3, cw), lambda bi, j: (0, j)),
                  pl.BlockSpec((1, cw), lambda bi, j: (0, j))],
        out_specs=pl.BlockSpec((1, s, cw), lambda bi, j: (bi, 0, j)),
        compiler_params=_cparams("parallel", "parallel"), name="conv_silu",
    )(xbc, w, b)


def _head_expander():
    return (jnp.arange(SSD_DI)[None, :] // SSD_P == jnp.arange(LANE)[:, None]).astype(BF16)


def _scan_prep(dt_ref, bias_ref, alog_ref, ex_ref, reverse):
    q = SSD_CHUNK
    dtr = dt_ref[0] + bias_ref[...]
    dt = jnp.maximum(dtr, 0.0) + jnp.log(1.0 + jnp.exp(-jnp.abs(dtr)))
    a = dt * (-jnp.exp(alog_ref[...]))
    li = lax.broadcasted_iota(jnp.int32, (q, q), 0)
    si = lax.broadcasted_iota(jnp.int32, (q, q), 1)
    seen = (si >= li) if reverse else (si <= li)
    tri = jnp.where(seen, 1.0, 0.0).astype(BF16)
    a1, a2, a3 = _split3(a)
    cs = _dot(tri, a1) + _dot(tri, a2) + _dot(tri, a3)
    cs_t = jnp.transpose(cs)
    last = 0 if reverse else q - 1
    tot = cs[last:last + 1, :]
    dt_t = jnp.transpose(dt)
    dec = jnp.broadcast_to(jnp.exp(tot), (8, LANE))
    dec_hi = dec.astype(BF16)
    dec_lo = (dec - dec_hi.astype(F32)).astype(BF16)
    fac = jnp.concatenate([(dt * jnp.exp(tot - cs)).astype(BF16), jnp.exp(cs).astype(BF16), dec_hi, dec_lo], axis=0)
    fac = _dot(fac, ex_ref[...])
    dec_x = fac[2 * q:2 * q + 1, :] + fac[2 * q + 8:2 * q + 9, :]
    return dict(seen=seen, cs=cs, cs_t=cs_t, dt_t=dt_t, fac=fac, dec_x=dec_x)


def _scan_group(ctx, g, x_ref, b_ref, c_ref, state_t, y_ref):
    q = SSD_CHUNK
    seen, cs, cs_t, dt_t, fac = ctx["seen"], ctx["cs"], ctx["cs_t"], ctx["dt_t"], ctx["fac"]
    low = lax.broadcasted_iota(jnp.int32, (q, LANE), 1) < SSD_P
    bg = b_ref[0, :, g * SSD_N:(g + 1) * SSD_N]
    cg = c_ref[0, :, g * SSD_N:(g + 1) * SSD_N]
    cb = _dot_nt(cg, bg)
    s_g = state_t[:, g * 512:(g + 1) * 512]
    y_off = _dot(cg, s_g.astype(BF16))
    xw_tiles = []
    for jj in range(4):
        j = g * 4 + jj
        cols = slice(j * LANE, (j + 1) * LANE)
        xb = x_ref[0, :, cols]
        ms = []
        for h in (2 * j, 2 * j + 1):
            seg = jnp.where(seen, cs[:, h:h + 1] - cs_t[h:h + 1, :], -jnp.inf)
            ms.append((cb * jnp.exp(seg) * dt_t[h:h + 1, :]).astype(BF16))
        zero = jnp.zeros_like(xb)
        rhs = jnp.concatenate([jnp.where(low, xb, zero), jnp.where(low, zero, xb)], axis=0)
        yt = y_off[:, jj * LANE:(jj + 1) * LANE] * fac[q:2 * q, cols] + _dot(jnp.concatenate(ms, axis=1), rhs)
        y_ref[0, :, cols] = yt.astype(y_ref.dtype)
        xw_tiles.append((xb.astype(F32) * fac[0:q, cols]).astype(BF16))
    xw = jnp.concatenate(xw_tiles, axis=1)
    bg_t = jnp.transpose(bg.astype(F32)).astype(BF16)
    state_t[:, g * 512:(g + 1) * 512] = s_g * ctx["dec_x"][:, g * 512:(g + 1) * 512] + _dot(bg_t, xw)


def _ssd_scan_kernel(xf_ref, bf_ref, cf_ref, dtf_ref, xb_ref, bb_ref, cb_ref, dtb_ref, biasf_ref, biasb_ref,
                     alogf_ref, alogb_ref, s0f_ref, s0b_ref, ex_ref, yf_ref, yb_ref, sff_ref, sfb_ref,
                     state_f, state_b):
    ci = pl.program_id(1)

    @pl.when(ci == 0)
    def _():
        state_f[...] = jnp.transpose(s0f_ref[0])
        state_b[...] = jnp.transpose(s0b_ref[0])

    fwd = _scan_prep(dtf_ref, biasf_ref, alogf_ref, ex_ref, False)
    bwd = _scan_prep(dtb_ref, biasb_ref, alogb_ref, ex_ref, True)
    for g in range(SSD_G):
        _scan_group(fwd, g, xf_ref, bf_ref, cf_ref, state_f, yf_ref)
        _scan_group(bwd, g, xb_ref, bb_ref, cb_ref, state_b, yb_ref)

    @pl.when(ci == pl.num_programs(1) - 1)
    def _():
        sff_ref[0] = jnp.transpose(state_f[...])
        sfb_ref[0] = jnp.transpose(state_b[...])


def _ssd_scan(xbc, dt, bias, alog, s0f, s0b):
    bsz, s, _ = xbc.shape
    nc = s // SSD_CHUNK
    rev = lambda ci: nc - 1 - ci
    chunk = lambda width, col, cmap: pl.BlockSpec((1, SSD_CHUNK, width), lambda bi, ci: (bi, cmap(ci), col))
    same = lambda ci: ci
    vec = lambda d: pl.BlockSpec((1, LANE), lambda bi, ci: (0, d))
    state = pl.BlockSpec((1, SSD_DI, SSD_N), lambda bi, ci: (bi, 0, 0))
    in_specs = [chunk(SSD_DI, 0, same), chunk(512, 4, same), chunk(512, 5, same), chunk(LANE, 0, same),
                chunk(SSD_DI, 0, rev), chunk(512, 4, rev), chunk(512, 5, rev), chunk(LANE, 1, rev),
                vec(0), vec(1), vec(0), vec(1), state, state,
                pl.BlockSpec((LANE, SSD_DI), lambda bi, ci: (0, 0))]
    y_shape = jax.ShapeDtypeStruct((bsz, s, SSD_DI), F32)
    s_shape = jax.ShapeDtypeStruct((bsz, SSD_DI, SSD_N), F32)
    return pl.pallas_call(
        _ssd_scan_kernel, out_shape=[y_shape, y_shape, s_shape, s_shape],
        grid=(bsz, nc), in_specs=in_specs,
        out_specs=[chunk(SSD_DI, 0, same), chunk(SSD_DI, 0, rev), state, state],
        scratch_shapes=[pltpu.VMEM((SSD_N, SSD_DI), F32), pltpu.VMEM((SSD_N, SSD_DI), F32)],
        compiler_params=_cparams("parallel", "arbitrary"), name="ssd_scan",
    )(xbc, xbc, xbc, dt, xbc, xbc, xbc, dt, bias, bias, alog, alog, s0f, s0b, _head_expander())


def _ssd_out_kernel(yf_ref, yb_ref, xs_ref, d_ref, z_ref, x_ref, mod_ref, ng_ref, w_ref, o_ref):
    y = yf_ref[...] + yb_ref[...] + xs_ref[...].astype(F32) * d_ref[...]
    y = y * _silu(z_ref[...].astype(F32))
    gw = SSD_DI // SSD_G
    parts = []
    for g in range(SSD_G):
        yg = y[:, g * gw:(g + 1) * gw]
        parts.append((yg * _rms(yg, gw) * ng_ref[:, g * gw:(g + 1) * gw]).astype(BF16))
    yn = jnp.concatenate(parts, axis=1)
    o_ref[...] = x_ref[...] + mod_ref[0, 2:3, :] * _dot(yn, w_ref[...])


def _ssd_out(yf, yb, xbc, dvec, z, x, mod, ng, w, *, seq, per_batch_mod, tm):
    t = x.shape[0]
    steps_per_seq = seq // tm
    mod_map = (lambda i: (i // steps_per_seq, 0, 0)) if per_batch_mod else (lambda i: (0, 0, 0))
    row = lambda n: pl.BlockSpec((tm, n), lambda i: (i, 0))
    full = lambda a: pl.BlockSpec(a.shape, lambda i: (0,) * a.ndim)
    return pl.pallas_call(
        _ssd_out_kernel,
        out_shape=jax.ShapeDtypeStruct((t, D), F32), grid=(t // tm,),
        in_specs=[row(SSD_DI), row(SSD_DI), row(SSD_DI), full(dvec), row(SSD_DI), row(D),
                  pl.BlockSpec((1, 6, D), mod_map), full(ng), full(w)],
        out_specs=row(D),
        compiler_params=_cparams("parallel"), name="ssd_out",
    )(yf, yb, xbc, dvec, z, x, mod, ng, w)


def _pad_chunks(w, n_chunks, width, to):
    lead = w.shape[:-1]
    w = w.reshape(lead + (n_chunks, width))
    w = jnp.pad(w, [(0, 0)] * len(lead) + [(0, 0), (0, to - width)])
    return w.reshape(lead + (n_chunks * to,))


def _attn_weights(w_in, a_qk_g, b_qa_g, b_wq_up, b_kva_g, b_qk_g):
    c = [0, 512, 1024, 1536, 1920, 2176, 2240]
    w_in_p = jnp.concatenate([
        _pad_chunks(w_in[:, c[0]:c[1]], 8, A_DQK, LANE), _pad_chunks(w_in[:, c[1]:c[2]], 8, A_DQK, LANE),
        w_in[:, c[2]:c[5]], _pad_chunks(w_in[:, c[5]:c[6]], 1, B_ROPE, LANE)], axis=1).astype(BF16)
    log2e = 1.0 / math.log(2.0)
    gq = jnp.tile(jnp.pad(a_qk_g[0] * (A_DQK ** -0.5 * log2e), (0, LANE - A_DQK)), 8)[None]
    gk = jnp.tile(jnp.pad(a_qk_g[1], (0, LANE - A_DQK)), 8)[None]
    wq = _pad_chunks(b_wq_up, B_HEADS, B_DQK, 256).astype(BF16)
    gbq = jnp.tile(jnp.pad(b_qk_g[0] * (B_DQK ** -0.5 * log2e), (0, 256 - B_DQK)), B_HEADS)[None]
    gbk = jnp.pad(b_qk_g[1], (0, 256 - B_DQK))[None]
    return dict(w_in=w_in_p, gq=gq, gk=gk, gqa=b_qa_g[None], wq=wq, gbq=gbq, gkva=b_kva_g[None]), gbk


def _tile(seq, cap):
    return min(seq, cap)


def _attn_layer(xp, xs, shapes, modp, mods, g1, wp, gbk, wkv, lam_p, gsub, w_out, ctx, lam_init):
    (bp, sp), (bs, ss) = shapes
    ctx_ka, ctx_va, ctx_ckv, ctx_kr = ctx
    past = ctx_ka.shape[1]
    tabs_s = _rope_tables(ss)
    tabs_p = _rope_tables(sp)
    tmp, tms = _tile(sp, 256), _tile(ss, 512)
    qa, ka, va, qb, ckv, kr = _attn_in(xp, modp, g1, wp, tabs_p, seq=sp, per_batch_mod=False, rope=False,
                                      kv_dtype=F32, tm=tmp)
    kb, vb = _mla_expand(ckv, kr, wkv, gbk, tabs_p, seq=sp, rope=False, tm=tmp)
    r3 = lambda a, b_, s_: a.reshape(b_, s_, a.shape[-1])
    o_p = _attention(r3(qa, bp, sp), r3(qb, bp, sp), lam_p, gsub,
                     [(r3(ka, bp, sp), r3(kb, bp, sp), _values_t(r3(va, bp, sp), r3(vb, bp, sp)))],
                     lam_init=lam_init, tq=_tile(sp, 256), nb=2 if bp % 2 == 0 else 1)
    new_ak = ka.reshape(bp, sp, 2 * A_HEADS, LANE)[..., :A_DQK].reshape(bp, 1, sp, A_HEADS, 2, A_DQK)
    new_av = va.reshape(bp, 1, sp, A_HEADS, A_DV)
    new_ckv = ckv.reshape(bp, 1, sp, B_KV_LORA)
    new_kr = kr[:, :B_ROPE].reshape(bp, 1, sp, B_ROPE)
    xp1 = _proj_res(o_p.reshape(bp * sp, D), xp, modp, w_out, seq=sp, per_batch_mod=False, gate_row=2, tm=tmp)
    qa, ka, va, qb, ckv, kr = _attn_in(xs, mods, g1, wp, tabs_s, seq=ss, per_batch_mod=True, rope=True,
                                      kv_dtype=BF16, tm=tms)
    kb, vb = _mla_expand(ckv, kr, wkv, gbk, tabs_s, seq=ss, rope=True, tm=tms)
    tpast = _tile(past, 512)
    tabs_c = _rope_tables(past)
    kb_c, vb_c = _mla_expand(ctx_ckv.reshape(bs * past, B_KV_LORA),
                             jnp.pad(ctx_kr.reshape(bs * past, B_ROPE), ((0, 0), (0, LANE - B_ROPE))),
                             wkv, gbk, tabs_c, seq=past, rope=False, tm=tpast)
    ka_c = _pad_chunks(ctx_ka.reshape(bs, past, 2 * A_HEADS * A_DQK), 2 * A_HEADS, A_DQK, LANE).astype(BF16)
    va_c = ctx_va.reshape(bs, past, A_HEADS * A_DV)
    o_s = _attention(r3(qa, bs, ss), r3(qb, bs, ss), lam_p, gsub,
                     [(ka_c, r3(kb_c, bs, past), _values_t(va_c, r3(vb_c, bs, past))),
                      (r3(ka, bs, ss), r3(kb, bs, ss), _values_t(r3(va, bs, ss), r3(vb, bs, ss)))],
                     lam_init=lam_init, tq=_tile(ss, 512))
    xs1 = _proj_res(o_s.reshape(bs * ss, D), xs, mods, w_out, seq=ss, per_batch_mod=True, gate_row=2, tm=tms)
    return xp1, xs1, (new_ak, new_av, new_ckv, new_kr)


def _ssd_stream(x, mod, g1, w_in, conv_w, conv_b, bias, alog, dvec, ng, w_out, s0f, s0b, *, bsz, seq, per_batch_mod, tm):
    z, xbc, dt = _ssd_in(x, mod, g1, w_in, seq=seq, per_batch_mod=per_batch_mod, tm=tm)
    xbc = _conv_silu(xbc.reshape(bsz, seq, SSD_CONV_DIM), conv_w, conv_b, cw=512)
    dt = dt.reshape(bsz, seq, 2 * LANE)
    yf, yb, sf, sb = _ssd_scan(xbc, dt, bias, alog, s0f, s0b)
    flat = lambda a: a.reshape(bsz * seq, a.shape[-1])
    x1 = _ssd_out(flat(yf), flat(yb), flat(xbc), dvec, z, x, mod, ng, w_out,
                  seq=seq, per_batch_mod=per_batch_mod, tm=tm)
    return x1, sf, sb


def kernel(x_prompt, x_sample, cache_a_k, cache_a_v, cache_b_ckv, cache_b_krope, state_ssd, c, c_ctx, mod_w, mod_b, norm1_g, norm2_g, attn_w_in, a_qk_g, a_lambda, a_sub_g, b_qa_g, b_wq_up, b_kva_g, b_wkv_up, b_qk_g, attn_w_out, ssd_w_in, ssd_conv_w, ssd_conv_b, ssd_dt_bias, ssd_a_log, ssd_d, ssd_norm_g, ssd_w_out, router_w, router_bias, moe_w_in, moe_w_out, shared_w_in, shared_w_out):
    bp, sp, _ = x_prompt.shape
    bs, ss, _ = x_sample.shape
    depth = mod_w.shape[0]
    shapes = ((bp, sp), (bs, ss))
    xp = x_prompt.reshape(bp * sp, D)
    xs = x_sample.reshape(bs * ss, D)

    n_c = 1 + bs
    rows = -(-n_c // 16) * 16
    c_all = jnp.pad(jnp.concatenate([c_ctx[None], c], axis=0), ((0, rows - n_c), (0, 0)))
    mod = _mod_vectors(c_all, mod_w, mod_b).reshape(depth, rows, 6, D)

    rw_t = jnp.pad(router_w.T, ((0, LANE - N_EXPERTS), (0, 0)))
    rb_col = jnp.pad(router_bias, (0, LANE - N_EXPERTS))[:, None]
    tmp, tms = _tile(sp, 256), _tile(ss, 512)

    new_caches = []
    new_ssd = []
    for l in range(depth):
        i = l // 2
        modp = mod[l, 0:1]
        mods = mod[l, 1:1 + bs]
        g1 = norm1_g[l][None]
        g2 = norm2_g[l][None]
        if l % 2 == 0:
            lam_init = 0.8 - 0.6 * math.exp(-0.3 * l)
            wp, gbk = _attn_weights(attn_w_in[i], a_qk_g[i], b_qa_g[i], b_wq_up[i], b_kva_g[i], b_qk_g[i])
            ctx = (cache_a_k[:, i], cache_a_v[:, i], cache_b_ckv[:, i], cache_b_krope[:, i])
            xp, xs, caches = _attn_layer(xp, xs, shapes, modp, mods, g1, wp, gbk, b_wkv_up[i].astype(BF16),
                                         a_lambda[i], a_sub_g[i][None], attn_w_out[i].astype(BF16), ctx, lam_init)
            new_caches.append(caches)
        else:
            w_in = ssd_w_in[i]
            n0 = SSD_DI + SSD_CONV_DIM
            w_in_p = jnp.concatenate([w_in[:, :n0], _pad_chunks(w_in[:, n0:], 2, SSD_H, LANE)], axis=1).astype(BF16)
            bias = _pad_chunks(ssd_dt_bias[i].reshape(1, 2 * SSD_H), 2, SSD_H, LANE)
            alog = _pad_chunks(ssd_a_log[i].reshape(1, 2 * SSD_H), 2, SSD_H, LANE)
            dvec = jnp.repeat(ssd_d[i], SSD_P)[None]
            args = (g1, w_in_p, ssd_conv_w[i], ssd_conv_b[i][None], bias, alog, dvec, ssd_norm_g[i][None],
                    ssd_w_out[i].astype(BF16))
            zero = jnp.zeros((bp, SSD_DI, SSD_N), F32)
            xp, sf, sb = _ssd_stream(xp, modp, *args, zero, zero, bsz=bp, seq=sp, per_batch_mod=False, tm=tmp)
            xs, _, _ = _ssd_stream(xs, mods, *args, state_ssd[:, i, 0].reshape(bs, SSD_DI, SSD_N),
                                   state_ssd[:, i, 1].reshape(bs, SSD_DI, SSD_N), bsz=bs, seq=ss,
                                   per_batch_mod=True, tm=tms)
            new_ssd.append(jnp.stack([sf, sb], axis=1).reshape(bp, 2, SSD_H, SSD_P, SSD_N))
        expert_w = (l, moe_w_in, moe_w_out, shared_w_in, shared_w_out)
        xp = _moe_block(xp, modp, g2, rw_t, rb_col, expert_w, seq=sp, per_batch_mod=False, tm=tmp)
        xs = _moe_block(xs, mods, g2, rw_t, rb_col, expert_w, seq=ss, per_batch_mod=True, tm=tms)

    cat = lambda k: jnp.concatenate([cc[k] for cc in new_caches], axis=1)
    return (xp.reshape(bp, sp, D), xs.reshape(bs, ss, D), cat(0), cat(1), cat(2), cat(3),
            jnp.stack(new_ssd, axis=1))
```

```python
import functools
import math

import jax
import jax.numpy as jnp
from jax import lax
from jax.experimental import pallas as pl
from jax.experimental.pallas import tpu as pltpu

F32 = jnp.float32
BF16 = jnp.bfloat16

D = 1024
EPS = 1e-6
GRID_W = 64
ROPE_BASE = 10000.0
LANE = 128

A_HEADS = 4
A_DQK = 64
A_DV = 128
B_HEADS = 4
B_NOPE = 128
B_ROPE = 64
B_DQK = B_NOPE + B_ROPE
B_DV = 128
B_Q_LORA = 384
B_KV_LORA = 256

SSD_DI = 2048
SSD_P = 64
SSD_H = 32
SSD_N = 128
SSD_G = 4
SSD_CHUNK = 128
SSD_CONV_DIM = SSD_DI + 2 * SSD_G * SSD_N

N_EXPERTS = 16
N_GROUPS = 4
EPG = 4
D_EXPERT = 256

VMEM_LIMIT = 56 * 1024 * 1024


def _cparams(*sem):
    return pltpu.CompilerParams(dimension_semantics=sem, vmem_limit_bytes=VMEM_LIMIT)


def _dot(a, b):
    return jnp.dot(a, b, preferred_element_type=F32)


def _dot_nt(a, b):
    return lax.dot_general(a, b, (((1,), (1,)), ((), ())), preferred_element_type=F32)


def _split3(x):
    x1 = x.astype(BF16)
    r = x - x1.astype(F32)
    x2 = r.astype(BF16)
    x3 = (r - x2.astype(F32)).astype(BF16)
    return x1, x2, x3


def _silu(x):
    return x * (1.0 / (1.0 + jnp.exp(-x)))


def _rms(x, n):
    return lax.rsqrt(jnp.sum(x * x, axis=-1, keepdims=True) * (1.0 / n) + EPS)


def _modulated(x, g, shift, scale):
    return x * _rms(x, D) * g * (1.0 + scale) + shift


def _mod_kernel(c_ref, w_ref, b_ref, o_ref):
    c = _silu(c_ref[...])
    w = w_ref[0]
    c1, c2, c3 = _split3(c)
    w1, w2, w3 = _split3(w)
    acc = _dot(c1, w1) + (_dot(c1, w2) + _dot(c2, w1)) + (_dot(c2, w2) + _dot(c1, w3) + _dot(c3, w1))
    o_ref[0] = acc + b_ref[0]


def _mod_vectors(c_all, mod_w, mod_b):
    depth = mod_w.shape[0]
    rows = c_all.shape[0]
    tn = 1536
    return pl.pallas_call(
        _mod_kernel,
        out_shape=jax.ShapeDtypeStruct((depth, rows, 6 * D), F32),
        grid=(depth, 6 * D // tn),
        in_specs=[
            pl.BlockSpec((rows, D), lambda l, j: (0, 0)),
            pl.BlockSpec((1, D, tn), lambda l, j: (l, 0, j)),
            pl.BlockSpec((1, 1, tn), lambda l, j: (l, 0, j)),
        ],
        out_specs=pl.BlockSpec((1, rows, tn), lambda l, j: (l, 0, j)),
        compiler_params=_cparams("parallel", "parallel"),
        name="mod_vectors",
    )(c_all, mod_w, mod_b.reshape(depth, 1, 6 * D))


def _rope_tables(n_tok):
    t = jnp.arange(n_tok)
    row = (t // GRID_W).astype(F32)
    col = (t % GRID_W).astype(F32)
    half = B_ROPE // 2
    inv = ROPE_BASE ** (-2.0 * jnp.arange(half // 2, dtype=F32) / half)
    ang = jnp.concatenate([row[:, None] * inv, col[:, None] * inv], axis=-1)
    cos, sin = jnp.cos(ang), jnp.sin(ang)
    zero = jnp.zeros((n_tok, LANE - 2 * half), F32)
    cos_t = jnp.concatenate([cos, cos, zero], axis=-1)
    sin_t = jnp.concatenate([-sin, sin, zero], axis=-1)
    return cos_t, sin_t


def _rope_chunk(c, cos, sin):
    lane = lax.broadcasted_iota(jnp.int32, c.shape, 1)
    swapped = jnp.where(lane < 32, pltpu.roll(c, LANE - 32, 1), pltpu.roll(c, 32, 1))
    return c * cos + swapped * sin


N_ATTN_COLS = 3328
_AQ0, _AK0, _AV0, _BQ0, _BKV0, _BKR0 = 0, 1024, 2048, 2560, 2944, 3200


def _attn_in_kernel(x_ref, mod_ref, g1_ref, w_ref, gq_ref, gk_ref, gqa_ref, wq_ref, gbq_ref, gkva_ref,
                    cos_ref, sin_ref, qa_ref, ka_ref, va_ref, qb_ref, ckv_ref, kr_ref, *, rope):
    x = x_ref[...]
    h = _modulated(x, g1_ref[...], mod_ref[0, 0:1, :], mod_ref[0, 1:2, :])
    z = _dot(h.astype(BF16), w_ref[...])
    if rope:
        cos = cos_ref[...]
        sin = sin_ref[...]
    for j in range(2 * A_HEADS):
        sl = slice(j * LANE, (j + 1) * LANE)
        q = z[:, _AQ0 + j * LANE:_AQ0 + (j + 1) * LANE]
        q = q * _rms(q, A_DQK) * gq_ref[:, sl]
        k = z[:, _AK0 + j * LANE:_AK0 + (j + 1) * LANE]
        k = k * _rms(k, A_DQK) * gk_ref[:, sl]
        if rope:
            q = _rope_chunk(q, cos, sin)
            k = _rope_chunk(k, cos, sin)
        qa_ref[:, sl] = q.astype(qa_ref.dtype)
        ka_ref[:, sl] = k.astype(ka_ref.dtype)
    va_ref[...] = z[:, _AV0:_AV0 + A_HEADS * A_DV].astype(va_ref.dtype)
    qc = z[:, _BQ0:_BQ0 + B_Q_LORA]
    qc = qc * _rms(qc, B_Q_LORA) * gqa_ref[...]
    qb = _dot(qc.astype(BF16), wq_ref[...])
    for hh in range(B_HEADS):
        lo = qb[:, hh * 256:hh * 256 + LANE]
        hi = qb[:, hh * 256 + LANE:(hh + 1) * 256]
        inv = lax.rsqrt((jnp.sum(lo * lo, axis=-1, keepdims=True)
                         + jnp.sum(hi * hi, axis=-1, keepdims=True)) * (1.0 / B_DQK) + EPS)
        lo = lo * inv * gbq_ref[:, hh * 256:hh * 256 + LANE]
        hi = hi * inv * gbq_ref[:, hh * 256 + LANE:(hh + 1) * 256]
        if rope:
            hi = _rope_chunk(hi, cos, sin)
        qb_ref[:, hh * 256:hh * 256 + LANE] = lo.astype(qb_ref.dtype)
        qb_ref[:, hh * 256 + LANE:(hh + 1) * 256] = hi.astype(qb_ref.dtype)
    kvc = z[:, _BKV0:_BKV0 + B_KV_LORA]
    ckv_ref[...] = kvc * _rms(kvc, B_KV_LORA) * gkva_ref[...]
    kr_ref[...] = z[:, _BKR0:_BKR0 + LANE]


def _attn_in(x, mod, g1, wp, rope_tabs, *, seq, per_batch_mod, rope, kv_dtype, tm):
    t = x.shape[0]
    steps_per_seq = seq // tm
    mod_map = (lambda i: (i // steps_per_seq, 0, 0)) if per_batch_mod else (lambda i: (0, 0, 0))
    tab_map = lambda i: (i % steps_per_seq, 0)
    row = lambda n: pl.BlockSpec((tm, n), lambda i: (i, 0))
    full = lambda a: pl.BlockSpec(a.shape, lambda i: (0,) * a.ndim)
    cos_t, sin_t = rope_tabs
    ins = [x, mod, g1, wp["w_in"], wp["gq"], wp["gk"], wp["gqa"], wp["wq"], wp["gbq"], wp["gkva"], cos_t, sin_t]
    in_specs = [row(D), pl.BlockSpec((1, 6, D), mod_map), full(g1), full(wp["w_in"]), full(wp["gq"]),
                full(wp["gk"]), full(wp["gqa"]), full(wp["wq"]), full(wp["gbq"]), full(wp["gkva"]),
                pl.BlockSpec((tm, LANE), tab_map), pl.BlockSpec((tm, LANE), tab_map)]
    out_shape = [
        jax.ShapeDtypeStruct((t, 1024), BF16),
        jax.ShapeDtypeStruct((t, 1024), kv_dtype),
        jax.ShapeDtypeStruct((t, 512), kv_dtype),
        jax.ShapeDtypeStruct((t, 1024), BF16),
        jax.ShapeDtypeStruct((t, B_KV_LORA), F32),
        jax.ShapeDtypeStruct((t, LANE), F32),
    ]
    out_specs = [row(1024), row(1024), row(512), row(1024), row(B_KV_LORA), row(LANE)]
    return pl.pallas_call(
        functools.partial(_attn_in_kernel, rope=rope),
        out_shape=out_shape, grid=(t // tm,), in_specs=in_specs, out_specs=out_specs,
        compiler_params=_cparams("parallel"), name="attn_in",
    )(*ins)


def _mla_expand_kernel(ckv_ref, kr_ref, w_ref, g_ref, cos_ref, sin_ref, kb_ref, vb_ref, *, rope):
    kv = _dot(ckv_ref[...].astype(BF16), w_ref[...])
    kr = kr_ref[...]
    kr_ss = jnp.sum(kr * kr, axis=-1, keepdims=True)
    for hh in range(B_HEADS):
        kn = kv[:, hh * 256:hh * 256 + LANE]
        inv = lax.rsqrt((jnp.sum(kn * kn, axis=-1, keepdims=True) + kr_ss) * (1.0 / B_DQK) + EPS)
        hi = kr * inv * g_ref[:, LANE:]
        if rope:
            hi = _rope_chunk(hi, cos_ref[...], sin_ref[...])
        kb_ref[:, hh * 256:hh * 256 + LANE] = (kn * inv * g_ref[:, :LANE]).astype(kb_ref.dtype)
        kb_ref[:, hh * 256 + LANE:(hh + 1) * 256] = hi.astype(kb_ref.dtype)
        vb_ref[:, hh * LANE:(hh + 1) * LANE] = kv[:, hh * 256 + LANE:(hh + 1) * 256].astype(vb_ref.dtype)


def _mla_expand(ckv, kr, wkv, gk, rope_tabs, *, seq, rope, tm):
    t = ckv.shape[0]
    steps_per_seq = seq // tm
    tab_map = lambda i: (i % steps_per_seq, 0)
    row = lambda n: pl.BlockSpec((tm, n), lambda i: (i, 0))
    full = lambda a: pl.BlockSpec(a.shape, lambda i: (0,) * a.ndim)
    cos_t, sin_t = rope_tabs
    return pl.pallas_call(
        functools.partial(_mla_expand_kernel, rope=rope),
        out_shape=[jax.ShapeDtypeStruct((t, 1024), BF16), jax.ShapeDtypeStruct((t, 512), BF16)],
        grid=(t // tm,),
        in_specs=[row(B_KV_LORA), row(LANE), full(wkv), full(gk),
                  pl.BlockSpec((tm, LANE), tab_map), pl.BlockSpec((tm, LANE), tab_map)],
        out_specs=[row(1024), row(512)],
        compiler_params=_cparams("parallel"), name="mla_expand",
    )(ckv, kr, wkv, gk, cos_t, sin_t)


VT_ROWS = A_DV + 16


def _values_t(va, vb):
    b, sk, _ = va.shape
    v = jnp.concatenate([va.reshape(b, sk, A_HEADS, A_DV), vb.reshape(b, sk, B_HEADS, B_DV)], axis=2)
    v = jnp.transpose(v.astype(BF16), (0, 2, 3, 1))
    return jnp.concatenate([v, jnp.ones((b, A_HEADS + B_HEADS, VT_ROWS - A_DV, sk), BF16)], axis=2)


def _attn_kernel(*refs, n_pieces, lam_init):
    qa_ref, qb_ref, lam_ref, gsub_ref = refs[:4]
    kv_refs = refs[4:4 + 3 * n_pieces]
    o_ref, s_scr, p_scr = refs[4 + 3 * n_pieces:]
    ka = kv_refs[0::3]
    kb = kv_refs[1::3]
    vt = kv_refs[2::3]
    lam_p = lam_ref[...]
    lam = (jnp.exp(jnp.sum(lam_p[0:1] * lam_p[1:2], axis=-1, keepdims=True))
           - jnp.exp(jnp.sum(lam_p[2:3] * lam_p[3:4], axis=-1, keepdims=True)) + lam_init)
    per_batch = 2 * A_HEADS + B_HEADS
    n_maps = per_batch * qa_ref.shape[0]

    def scores(n):
        bb, j = divmod(n, per_batch)
        r0 = 0
        for i in range(n_pieces):
            if j < 2 * A_HEADS:
                sl = slice(j * LANE, (j + 1) * LANE)
                s = _dot_nt(ka[i][bb, :, sl].astype(BF16), qa_ref[bb, :, sl])
            else:
                sl = slice((j - 2 * A_HEADS) * 256, (j - 2 * A_HEADS + 1) * 256)
                s = _dot_nt(kb[i][bb, :, sl], qb_ref[bb, :, sl])
            s_scr[n % 2, r0:r0 + s.shape[0], :] = s
            r0 += s.shape[0]

    def attend(n):
        bb, j = divmod(n, per_batch)
        head = j // 2 if j < 2 * A_HEADS else j - A_HEADS
        mx = jnp.max(s_scr[n % 2], axis=0, keepdims=True)
        p_scr[n % 2] = jnp.exp2(s_scr[n % 2] - mx).astype(BF16)
        acc = None
        r0 = 0
        for i in range(n_pieces):
            v = vt[i][bb, head]
            part = _dot(v, p_scr[n % 2, r0:r0 + v.shape[1], :])
            acc = part if acc is None else acc + part
            r0 += v.shape[1]
        return acc[:A_DV] / acc[A_DV:A_DV + 1]

    scores(0)
    first = None
    for n in range(n_maps):
        if n + 1 < n_maps:
            scores(n + 1)
        out = attend(n)
        bb, j = divmod(n, per_batch)
        if j >= 2 * A_HEADS:
            c0 = A_HEADS * A_DV + (j - 2 * A_HEADS) * B_DV
            o_ref[bb, :, c0:c0 + B_DV] = jnp.transpose(out).astype(o_ref.dtype)
        elif j % 2 == 0:
            first = out
        else:
            o = jnp.transpose(first - lam * out)
            o = o * _rms(o, A_DV) * gsub_ref[...] * (1.0 - lam_init)
            o_ref[bb, :, (j // 2) * A_DV:(j // 2 + 1) * A_DV] = o.astype(o_ref.dtype)


def _attention(qa, qb, lam_p, gsub, pieces, *, lam_init, tq, nb=1):
    b, s, _ = qa.shape
    qspec = pl.BlockSpec((nb, tq, 1024), lambda bi, qi: (bi, qi, 0))
    full = lambda a: pl.BlockSpec(a.shape, lambda bi, qi: (0,) * a.ndim)
    ins = [qa, qb, lam_p, gsub]
    in_specs = [qspec, qspec, full(lam_p), full(gsub)]
    kv_mode = dict(pipeline_mode=pl.Buffered(1)) if s // tq > 1 else {}
    for piece in pieces:
        for a in piece:
            ins.append(a)
            in_specs.append(pl.BlockSpec((nb,) + a.shape[1:], lambda bi, qi, nd=a.ndim: (bi,) + (0,) * (nd - 1),
                                         **kv_mode))
    sk = sum(piece[0].shape[1] for piece in pieces)
    return pl.pallas_call(
        functools.partial(_attn_kernel, n_pieces=len(pieces), lam_init=lam_init),
        out_shape=jax.ShapeDtypeStruct((b, s, 1024), BF16),
        grid=(b // nb, s // tq), in_specs=in_specs,
        out_specs=pl.BlockSpec((nb, tq, 1024), lambda bi, qi: (bi, qi, 0)),
        scratch_shapes=[pltpu.VMEM((2, sk, tq), F32), pltpu.VMEM((2, sk, tq), BF16)],
        compiler_params=_cparams("parallel", "arbitrary"), name="attn_core",
    )(*ins)


def _proj_res_kernel(o_ref, x_ref, mod_ref, w_ref, y_ref, *, gate_row):
    y_ref[...] = x_ref[...] + mod_ref[0, gate_row:gate_row + 1, :] * _dot(o_ref[...], w_ref[...])


def _proj_res(o, x, mod, w, *, seq, per_batch_mod, gate_row, tm):
    t, k = o.shape
    steps_per_seq = seq // tm
    mod_map = (lambda i: (i // steps_per_seq, 0, 0)) if per_batch_mod else (lambda i: (0, 0, 0))
    return pl.pallas_call(
        functools.partial(_proj_res_kernel, gate_row=gate_row),
        out_shape=jax.ShapeDtypeStruct((t, D), F32), grid=(t // tm,),
        in_specs=[pl.BlockSpec((tm, k), lambda i: (i, 0)), pl.BlockSpec((tm, D), lambda i: (i, 0)),
                  pl.BlockSpec((1, 6, D), mod_map), pl.BlockSpec(w.shape, lambda i: (0, 0))],
        out_specs=pl.BlockSpec((tm, D), lambda i: (i, 0)),
        compiler_params=_cparams("parallel"), name="proj_res",
    )(o, x, mod, w)


def _route(logits_t, bias_col):
    tm = logits_t.shape[1]
    scores = 1.0 / (1.0 + jnp.exp(-logits_t[:N_EXPERTS]))
    sel = scores + bias_col[:N_EXPERTS]
    grp = []
    for g in range(N_GROUPS):
        r = [sel[EPG * g + j:EPG * g + j + 1] for j in range(EPG)]
        best2 = None
        for a in range(EPG):
            for b2 in range(a + 1, EPG):
                pair = r[a] + r[b2]
                best2 = pair if best2 is None else jnp.maximum(best2, pair)
        grp.append(best2)
    best = jnp.zeros((1, tm), jnp.int32)
    best_v = grp[0]
    for g in range(1, N_GROUPS):
        better = grp[g] > best_v
        best = jnp.where(better, g, best)
        best_v = jnp.where(better, grp[g], best_v)
    v = []
    sc = []
    for j in range(EPG):
        vj = sel[j:j + 1]
        sj = scores[j:j + 1]
        for g in range(1, N_GROUPS):
            vj = jnp.where(best == g, sel[EPG * g + j:EPG * g + j + 1], vj)
            sj = jnp.where(best == g, scores[EPG * g + j:EPG * g + j + 1], sj)
        v.append(vj)
        sc.append(sj)
    picked = []
    for j in range(EPG):
        rank = jnp.zeros((1, tm), jnp.int32)
        for k in range(EPG):
            if k == j:
                continue
            ahead = (v[k] >= v[j]) if k < j else (v[k] > v[j])
            rank = rank + ahead.astype(jnp.int32)
        picked.append(rank < 2)
    den = functools.reduce(jnp.add, [jnp.where(picked[j], sc[j], 0.0) for j in range(EPG)])
    gates = [jnp.where(picked[j], sc[j] / den, 0.0) for j in range(EPG)]
    return best, gates


ROW_TILES = D // LANE


def _store_token_tiles(ref, x):
    n = x.shape[0]
    for c in range(ROW_TILES):
        ref[pl.ds(c, n, stride=ROW_TILES), :] = x[:, c * LANE:(c + 1) * LANE]


def _load_token_tiles(ref, n):
    return jnp.concatenate([ref[pl.ds(c, n, stride=ROW_TILES), :] for c in range(ROW_TILES)], axis=1)


def _moe_route_kernel(x_ref, mod_ref, g2_ref, rw_ref, rb_ref, upper_ref, h_ref, info_ref, carry):
    tm = x_ref.shape[0]

    @pl.when(pl.program_id(0) == 0)
    def _():
        carry[...] = jnp.zeros_like(carry)

    h = _modulated(x_ref[...], g2_ref[...], mod_ref[0, 3:4, :], mod_ref[0, 4:5, :])
    _store_token_tiles(h_ref, h)
    h1, h2, h3 = _split3(h)
    w1, w2, w3 = _split3(rw_ref[...])
    logits_t = (_dot_nt(w1, h1) + (_dot_nt(w1, h2) + _dot_nt(w2, h1))
                + (_dot_nt(w2, h2) + _dot_nt(w1, h3) + _dot_nt(w3, h1)))
    best, gates = _route(logits_t, rb_ref[...])
    grow = lax.broadcasted_iota(jnp.int32, (8, tm), 0)
    onehot = jnp.where(grow == best, 1.0, 0.0)
    cum = _dot(onehot.astype(BF16), upper_ref[...]) + carry[:, 0:1]
    rank = jnp.sum(onehot * cum, axis=0, keepdims=True) - 1.0
    carry[...] = jnp.broadcast_to(cum[:, tm - 1:tm], carry.shape)
    info_ref[...] = jnp.concatenate(gates + [best.astype(F32), rank, jnp.zeros((2, tm), F32)], axis=0)


def _moe_route(x, mod, g2, rw_t, rb_col, *, seq, per_batch_mod, tm):
    t = x.shape[0]
    steps_per_seq = seq // tm
    mod_map = (lambda i: (i // steps_per_seq, 0, 0)) if per_batch_mod else (lambda i: (0, 0, 0))
    full = lambda a: pl.BlockSpec(a.shape, lambda i: (0,) * a.ndim)
    upper = (jnp.arange(tm)[:, None] <= jnp.arange(tm)[None, :]).astype(BF16)
    return pl.pallas_call(
        _moe_route_kernel,
        out_shape=[jax.ShapeDtypeStruct((t * ROW_TILES, LANE), F32), jax.ShapeDtypeStruct((8, t), F32)],
        grid=(t // tm,),
        in_specs=[pl.BlockSpec((tm, D), lambda i: (i, 0)), pl.BlockSpec((1, 6, D), mod_map), full(g2),
                  full(rw_t), full(rb_col), full(upper)],
        out_specs=[pl.BlockSpec((tm * ROW_TILES, LANE), lambda i: (i, 0)), pl.BlockSpec((8, tm), lambda i: (0, i))],
        scratch_shapes=[pltpu.VMEM((8, LANE), F32)],
        compiler_params=_cparams("arbitrary"), name="moe_route",
    )(x, mod, g2, rw_t, rb_col, upper)


def _dispatch_plan(info, t, tm):
    grp = info[4].astype(jnp.int32)
    rank = info[5].astype(jnp.int32)
    n_slots = t + N_GROUPS * tm
    n_tiles = n_slots // tm
    counts = [jnp.sum((grp == g).astype(jnp.int32)) for g in range(N_GROUPS)]
    tiles = [(c + tm - 1) // tm for c in counts]
    start_tile = [sum(tiles[:g], jnp.int32(0)) for g in range(N_GROUPS + 1)]
    dest = rank + sum(jnp.where(grp == g, start_tile[g] * tm, 0) for g in range(N_GROUPS))
    n_fill = n_slots - t
    k = jnp.arange(n_fill, dtype=jnp.int32)
    fill_key = jnp.zeros_like(k)
    seen = jnp.int32(0)
    for g in range(N_GROUPS):
        pad = tiles[g] * tm - counts[g]
        fill_key = jnp.where((k >= seen) & (k < seen + pad), start_tile[g] * tm + counts[g] + (k - seen), fill_key)
        seen = seen + pad
    fill_key = jnp.where(k >= seen, start_tile[N_GROUPS] * tm + (k - seen), fill_key)
    keys = jnp.concatenate([dest, fill_key])
    tok = jnp.concatenate([jnp.arange(t, dtype=jnp.int32), jnp.full((n_fill,), t, jnp.int32)])
    gate_rows = [jnp.concatenate([info[j], jnp.zeros((n_fill,), F32)]) for j in range(EPG)]
    _, tok_s, *gate_s = lax.sort((keys, tok, *gate_rows), num_keys=1)
    valid = tok_s < t
    slot = jnp.arange(n_slots, dtype=jnp.int32)
    dump = t + ((slot // tm) % 2) * tm + slot % tm
    src = jnp.where(valid, tok_s, 0)
    dst = jnp.where(valid, tok_s, dump)
    tile_idx = jnp.arange(n_tiles, dtype=jnp.int32)
    tile_group = sum((tile_idx >= start_tile[g]).astype(jnp.int32) for g in range(1, N_GROUPS))
    return (tile_group, src.reshape(n_tiles, 1, tm), dst.reshape(n_tiles, 1, tm), jnp.stack(gate_s, axis=1))


def _moe_sorted_kernel(tg_ref, src_ref, srcn_ref, dst_ref, gate_ref, win_ref, wout_ref, shin_ref, shout_ref,
                       h_hbm, y_hbm, hbuf, ybuf, win_bf, wout_bf, gsem, ssem, *, tm, n_tiles):
    i = pl.program_id(0)
    slot = lax.rem(i, 2)

    rt = ROW_TILES

    def token_rows(tok):
        return pl.ds(pl.multiple_of(tok * rt, rt), rt)

    def gather(idx_ref, s):
        for r in range(tm):
            pltpu.make_async_copy(h_hbm.at[token_rows(idx_ref[0, 0, r])], hbuf.at[s, pl.ds(r * rt, rt)],
                                  gsem.at[s]).start(priority=r % 2)

    def gather_wait(s):
        pltpu.make_async_copy(h_hbm.at[pl.ds(0, tm * rt)], hbuf.at[s], gsem.at[s]).wait()

    def scatter_wait(s):
        pltpu.make_async_copy(ybuf.at[s], y_hbm.at[pl.ds(0, tm * rt)], ssem.at[s]).wait()

    @pl.when(i == 0)
    def _():
        gather(src_ref, 0)

    @pl.when(i + 1 < n_tiles)
    def _():
        gather(srcn_ref, 1 - slot)

    @pl.when(i == 0)
    def _():
        win_bf[EPG] = shin_ref[...].astype(BF16)
        wout_bf[EPG * D_EXPERT:, :] = shout_ref[...].astype(BF16)

    @pl.when((i == 0) | (tg_ref[i] != tg_ref[jnp.maximum(i - 1, 0)]))
    def _():
        for e in range(EPG):
            win_bf[e] = win_ref[e].astype(BF16)
            wout_bf[e * D_EXPERT:(e + 1) * D_EXPERT, :] = wout_ref[e].astype(BF16)

    gather_wait(slot)
    x = _load_token_tiles(hbuf.at[slot], tm).astype(BF16)
    parts = []
    for e in range(EPG + 1):
        ue = _dot(x, win_bf[e])
        he = _silu(ue[:, :D_EXPERT]) * ue[:, D_EXPERT:]
        if e < EPG:
            he = he * gate_ref[:, e:e + 1]
        parts.append(he.astype(BF16))
    y = _dot(jnp.concatenate(parts, axis=1), wout_bf[...])

    @pl.when(i >= 2)
    def _():
        scatter_wait(slot)

    _store_token_tiles(ybuf.at[slot], y)
    for r in range(tm):
        pltpu.make_async_copy(ybuf.at[slot, pl.ds(r * rt, rt)], y_hbm.at[token_rows(dst_ref[0, 0, r])],
                              ssem.at[slot]).start(priority=r % 2)

    @pl.when(i == n_tiles - 1)
    def _():
        if n_tiles >= 2:
            scatter_wait(1 - slot)
        scatter_wait(slot)


def _moe_sorted(h, plan, layer, w_in, w_out, sh_in, sh_out, *, tm):
    t = h.shape[0] // ROW_TILES
    tile_group, src, dst, gates = plan
    n_tiles = src.shape[0]
    smem_blk = lambda imap: pl.BlockSpec((1, 1, tm), imap, memory_space=pltpu.SMEM)
    grid_spec = pltpu.PrefetchScalarGridSpec(
        num_scalar_prefetch=1, grid=(n_tiles,),
        in_specs=[smem_blk(lambda i, tg: (i, 0, 0)),
                  smem_blk(lambda i, tg: (jnp.minimum(i + 1, n_tiles - 1), 0, 0)),
                  smem_blk(lambda i, tg: (i, 0, 0)),
                  pl.BlockSpec((tm, EPG), lambda i, tg: (i, 0)),
                  pl.BlockSpec((None, EPG, D, 2 * D_EXPERT), lambda i, tg: (layer, tg[i], 0, 0)),
                  pl.BlockSpec((None, EPG, D_EXPERT, D), lambda i, tg: (layer, tg[i], 0, 0)),
                  pl.BlockSpec((None, D, 2 * D_EXPERT), lambda i, tg: (layer, 0, 0)),
                  pl.BlockSpec((None, D_EXPERT, D), lambda i, tg: (layer, 0, 0)),
                  pl.BlockSpec(memory_space=pl.ANY)],
        out_specs=pl.BlockSpec(memory_space=pl.ANY),
        scratch_shapes=[pltpu.VMEM((2, tm * ROW_TILES, LANE), F32), pltpu.VMEM((2, tm * ROW_TILES, LANE), F32),
                        pltpu.VMEM((EPG + 1, D, 2 * D_EXPERT), BF16), pltpu.VMEM(((EPG + 1) * D_EXPERT, D), BF16),
                        pltpu.SemaphoreType.DMA((2,)), pltpu.SemaphoreType.DMA((2,))])
    return pl.pallas_call(
        functools.partial(_moe_sorted_kernel, tm=tm, n_tiles=n_tiles),
        out_shape=jax.ShapeDtypeStruct(((t + 2 * tm) * ROW_TILES, LANE), F32), grid_spec=grid_spec,
        compiler_params=_cparams("arbitrary"), name="moe_sorted",
    )(tile_group, src, src, dst, gates, w_in, w_out, sh_in, sh_out, h)


def _moe_residual_kernel(x_ref, y_ref, mod_ref, o_ref):
    o_ref[...] = x_ref[...] + mod_ref[0, 5:6, :] * _load_token_tiles(y_ref, x_ref.shape[0])


def _moe_residual(x, y, mod, *, seq, per_batch_mod, tm):
    t = x.shape[0]
    steps_per_seq = seq // tm
    mod_map = (lambda i: (i // steps_per_seq, 0, 0)) if per_batch_mod else (lambda i: (0, 0, 0))
    row = pl.BlockSpec((tm, D), lambda i: (i, 0))
    return pl.pallas_call(
        _moe_residual_kernel, out_shape=jax.ShapeDtypeStruct((t, D), F32), grid=(t // tm,),
        in_specs=[row, pl.BlockSpec((tm * ROW_TILES, LANE), lambda i: (i, 0)), pl.BlockSpec((1, 6, D), mod_map)],
        out_specs=row,
        compiler_params=_cparams("parallel"), name="moe_residual",
    )(x, y, mod)


MOE_TILE = 256


def _moe_block(x, mod, g2, rw_t, rb_col, expert_w, *, seq, per_batch_mod, tm):
    h, info = _moe_route(x, mod, g2, rw_t, rb_col, seq=seq, per_batch_mod=per_batch_mod, tm=tm)
    plan = _dispatch_plan(info, x.shape[0], MOE_TILE)
    y = _moe_sorted(h, plan, *expert_w, tm=MOE_TILE)
    return _moe_residual(x, y, mod, seq=seq, per_batch_mod=per_batch_mod, tm=tm)


SSD_IN_COLS = SSD_DI + SSD_CONV_DIM + 2 * LANE


def _ssd_in_kernel(x_ref, mod_ref, g1_ref, w_ref, z_ref, xbc_ref, dt_ref):
    h = _modulated(x_ref[...], g1_ref[...], mod_ref[0, 0:1, :], mod_ref[0, 1:2, :])
    hb = h.astype(BF16)
    z_ref[...] = _dot(hb, w_ref[:, :SSD_DI]).astype(z_ref.dtype)
    xbc_ref[...] = _dot(hb, w_ref[:, SSD_DI:SSD_DI + SSD_CONV_DIM]).astype(xbc_ref.dtype)
    dt_ref[...] = _dot(hb, w_ref[:, SSD_DI + SSD_CONV_DIM:])


def _ssd_in(x, mod, g1, w, *, seq, per_batch_mod, tm):
    t = x.shape[0]
    steps_per_seq = seq // tm
    mod_map = (lambda i: (i // steps_per_seq, 0, 0)) if per_batch_mod else (lambda i: (0, 0, 0))
    row = lambda n: pl.BlockSpec((tm, n), lambda i: (i, 0))
    return pl.pallas_call(
        _ssd_in_kernel,
        out_shape=[jax.ShapeDtypeStruct((t, SSD_DI), BF16), jax.ShapeDtypeStruct((t, SSD_CONV_DIM), F32),
                   jax.ShapeDtypeStruct((t, 2 * LANE), F32)],
        grid=(t // tm,),
        in_specs=[row(D), pl.BlockSpec((1, 6, D), mod_map), pl.BlockSpec(g1.shape, lambda i: (0, 0)),
                  pl.BlockSpec(w.shape, lambda i: (0, 0))],
        out_specs=[row(SSD_DI), row(SSD_CONV_DIM), row(2 * LANE)],
        compiler_params=_cparams("parallel"), name="ssd_in",
    )(x, mod, g1, w)


def _conv_kernel(x_ref, w_ref, b_ref, o_ref):
    x = x_ref[0]
    s = x.shape[0]

    def taps(prev, cur, nxt):
        y = prev * w_ref[0:1, :] + cur * w_ref[1:2, :] + nxt * w_ref[2:3, :] + b_ref[...]
        return _silu(y).astype(o_ref.dtype)

    o_ref[0] = taps(pltpu.roll(x, 1, 0), x, pltpu.roll(x, s - 1, 0))
    rowi = lax.broadcasted_iota(jnp.int32, (32, x.shape[1]), 0)
    top = x[0:32]
    o_ref[0, 0:16, :] = taps(jnp.where(rowi == 0, 0.0, pltpu.roll(top, 1, 0)), top, pltpu.roll(top, 31, 0))[0:16]
    bot = x[s - 32:s]
    o_ref[0, s - 16:s, :] = taps(pltpu.roll(bot, 1, 0), bot,
                                 jnp.where(rowi == 31, 0.0, pltpu.roll(bot, 31, 0)))[16:32]


def _conv_silu(xbc, w, b, *, cw):
    bsz, s, c = xbc.shape
    return pl.pallas_call(
        _conv_kernel,
        out_shape=jax.ShapeDtypeStruct((bsz, s, c), BF16), grid=(bsz, c // cw),
        in_specs=[pl.BlockSpec((1, s, cw), lambda bi, j: (bi, 0, j)), pl.BlockSpec((3, cw), lambda bi, j: (0, j)),
                  pl.BlockSpec((1, cw), lambda bi, j: (0, j))],
        out_specs=pl.BlockSpec((1, s, cw), lambda bi, j: (bi, 0, j)),
        compiler_params=_cparams("parallel", "parallel"), name="conv_silu",
    )(xbc, w, b)


def _head_expander():
    return (jnp.arange(SSD_DI)[None, :] // SSD_P == jnp.arange(LANE)[:, None]).astype(BF16)


def _scan_prep(dt_ref, bias_ref, alog_ref, ex_ref, reverse):
    q = SSD_CHUNK
    dtr = dt_ref[0] + bias_ref[...]
    dt = jnp.maximum(dtr, 0.0) + jnp.log(1.0 + jnp.exp(-jnp.abs(dtr)))
    a = dt * (-jnp.exp(alog_ref[...]))
    li = lax.broadcasted_iota(jnp.int32, (q, q), 0)
    si = lax.broadcasted_iota(jnp.int32, (q, q), 1)
    seen = (si >= li) if reverse else (si <= li)
    tri = jnp.where(seen, 1.0, 0.0).astype(BF16)
    a1, a2, a3 = _split3(a)
    cs = _dot(tri, a1) + _dot(tri, a2) + _dot(tri, a3)
    cs_t = jnp.transpose(cs)
    last = 0 if reverse else q - 1
    tot = cs[last:last + 1, :]
    dt_t = jnp.transpose(dt)
    dec = jnp.broadcast_to(jnp.exp(tot), (8, LANE))
    dec_hi = dec.astype(BF16)
    dec_lo = (dec - dec_hi.astype(F32)).astype(BF16)
    fac = jnp.concatenate([(dt * jnp.exp(tot - cs)).astype(BF16), jnp.exp(cs).astype(BF16), dec_hi, dec_lo], axis=0)
    fac = _dot(fac, ex_ref[...])
    dec_x = fac[2 * q:2 * q + 1, :] + fac[2 * q + 8:2 * q + 9, :]
    return dict(seen=seen, cs=cs, cs_t=cs_t, dt_t=dt_t, fac=fac, dec_x=dec_x)


def _scan_group(ctx, g, x_ref, b_ref, c_ref, state_t, y_ref):
    q = SSD_CHUNK
    seen, cs, cs_t, dt_t, fac = ctx["seen"], ctx["cs"], ctx["cs_t"], ctx["dt_t"], ctx["fac"]
    low = lax.broadcasted_iota(jnp.int32, (q, LANE), 1) < SSD_P
    bg = b_ref[0, :, g * SSD_N:(g + 1) * SSD_N]
    cg = c_ref[0, :, g * SSD_N:(g + 1) * SSD_N]
    cb = _dot_nt(cg, bg)
    s_g = state_t[:, g * 512:(g + 1) * 512]
    y_off = _dot(cg, s_g.astype(BF16))
    xw_tiles = []
    for jj in range(4):
        j = g * 4 + jj
        cols = slice(j * LANE, (j + 1) * LANE)
        xb = x_ref[0, :, cols]
        ms = []
        for h in (2 * j, 2 * j + 1):
            seg = jnp.where(seen, cs[:, h:h + 1] - cs_t[h:h + 1, :], -jnp.inf)
            ms.append((cb * jnp.exp(seg) * dt_t[h:h + 1, :]).astype(BF16))
        zero = jnp.zeros_like(xb)
        rhs = jnp.concatenate([jnp.where(low, xb, zero), jnp.where(low, zero, xb)], axis=0)
        yt = y_off[:, jj * LANE:(jj + 1) * LANE] * fac[q:2 * q, cols] + _dot(jnp.concatenate(ms, axis=1), rhs)
        y_ref[0, :, cols] = yt.astype(y_ref.dtype)
        xw_tiles.append((xb.astype(F32) * fac[0:q, cols]).astype(BF16))
    xw = jnp.concatenate(xw_tiles, axis=1)
    bg_t = jnp.transpose(bg.astype(F32)).astype(BF16)
    state_t[:, g * 512:(g + 1) * 512] = s_g * ctx["dec_x"][:, g * 512:(g + 1) * 512] + _dot(bg_t, xw)


def _ssd_scan_kernel(xf_ref, bf_ref, cf_ref, dtf_ref, xb_ref, bb_ref, cb_ref, dtb_ref, biasf_ref, biasb_ref,
                     alogf_ref, alogb_ref, s0f_ref, s0b_ref, ex_ref, yf_ref, yb_ref, sff_ref, sfb_ref,
                     state_f, state_b):
    ci = pl.program_id(1)

    @pl.when(ci == 0)
    def _():
        state_f[...] = jnp.transpose(s0f_ref[0])
        state_b[...] = jnp.transpose(s0b_ref[0])

    fwd = _scan_prep(dtf_ref, biasf_ref, alogf_ref, ex_ref, False)
    bwd = _scan_prep(dtb_ref, biasb_ref, alogb_ref, ex_ref, True)
    for g in range(SSD_G):
        _scan_group(fwd, g, xf_ref, bf_ref, cf_ref, state_f, yf_ref)
        _scan_group(bwd, g, xb_ref, bb_ref, cb_ref, state_b, yb_ref)

    @pl.when(ci == pl.num_programs(1) - 1)
    def _():
        sff_ref[0] = jnp.transpose(state_f[...])
        sfb_ref[0] = jnp.transpose(state_b[...])


def _ssd_scan(xbc, dt, bias, alog, s0f, s0b):
    bsz, s, _ = xbc.shape
    nc = s // SSD_CHUNK
    rev = lambda ci: nc - 1 - ci
    chunk = lambda width, col, cmap: pl.BlockSpec((1, SSD_CHUNK, width), lambda bi, ci: (bi, cmap(ci), col))
    same = lambda ci: ci
    vec = lambda d: pl.BlockSpec((1, LANE), lambda bi, ci: (0, d))
    state = pl.BlockSpec((1, SSD_DI, SSD_N), lambda bi, ci: (bi, 0, 0))
    in_specs = [chunk(SSD_DI, 0, same), chunk(512, 4, same), chunk(512, 5, same), chunk(LANE, 0, same),
                chunk(SSD_DI, 0, rev), chunk(512, 4, rev), chunk(512, 5, rev), chunk(LANE, 1, rev),
                vec(0), vec(1), vec(0), vec(1), state, state,
                pl.BlockSpec((LANE, SSD_DI), lambda bi, ci: (0, 0))]
    y_shape = jax.ShapeDtypeStruct((bsz, s, SSD_DI), BF16)
    s_shape = jax.ShapeDtypeStruct((bsz, SSD_DI, SSD_N), F32)
    return pl.pallas_call(
        _ssd_scan_kernel, out_shape=[y_shape, y_shape, s_shape, s_shape],
        grid=(bsz, nc), in_specs=in_specs,
        out_specs=[chunk(SSD_DI, 0, same), chunk(SSD_DI, 0, rev), state, state],
        scratch_shapes=[pltpu.VMEM((SSD_N, SSD_DI), F32), pltpu.VMEM((SSD_N, SSD_DI), F32)],
        compiler_params=_cparams("parallel", "arbitrary"), name="ssd_scan",
    )(xbc, xbc, xbc, dt, xbc, xbc, xbc, dt, bias, bias, alog, alog, s0f, s0b, _head_expander())


def _ssd_out_kernel(yf_ref, yb_ref, xs_ref, d_ref, z_ref, x_ref, mod_ref, ng_ref, w_ref, o_ref):
    y = yf_ref[...].astype(F32) + yb_ref[...].astype(F32) + xs_ref[...].astype(F32) * d_ref[...]
    y = y * _silu(z_ref[...].astype(F32))
    gw = SSD_DI // SSD_G
    parts = []
    for g in range(SSD_G):
        yg = y[:, g * gw:(g + 1) * gw]
        parts.append((yg * _rms(yg, gw) * ng_ref[:, g * gw:(g + 1) * gw]).astype(BF16))
    yn = jnp.concatenate(parts, axis=1)
    o_ref[...] = x_ref[...] + mod_ref[0, 2:3, :] * _dot(yn, w_ref[...])


def _ssd_out(yf, yb, xbc, dvec, z, x, mod, ng, w, *, seq, per_batch_mod, tm):
    t = x.shape[0]
    steps_per_seq = seq // tm
    mod_map = (lambda i: (i // steps_per_seq, 0, 0)) if per_batch_mod else (lambda i: (0, 0, 0))
    row = lambda n: pl.BlockSpec((tm, n), lambda i: (i, 0))
    full = lambda a: pl.BlockSpec(a.shape, lambda i: (0,) * a.ndim)
    return pl.pallas_call(
        _ssd_out_kernel,
        out_shape=jax.ShapeDtypeStruct((t, D), F32), grid=(t // tm,),
        in_specs=[row(SSD_DI), row(SSD_DI), row(SSD_DI), full(dvec), row(SSD_DI), row(D),
                  pl.BlockSpec((1, 6, D), mod_map), full(ng), full(w)],
        out_specs=row(D),
        compiler_params=_cparams("parallel"), name="ssd_out",
    )(yf, yb, xbc, dvec, z, x, mod, ng, w)


def _pad_chunks(w, n_chunks, width, to):
    lead = w.shape[:-1]
    w = w.reshape(lead + (n_chunks, width))
    w = jnp.pad(w, [(0, 0)] * len(lead) + [(0, 0), (0, to - width)])
    return w.reshape(lead + (n_chunks * to,))


def _attn_weights(w_in, a_qk_g, b_qa_g, b_wq_up, b_kva_g, b_qk_g):
    c = [0, 512, 1024, 1536, 1920, 2176, 2240]
    w_in_p = jnp.concatenate([
        _pad_chunks(w_in[:, c[0]:c[1]], 8, A_DQK, LANE), _pad_chunks(w_in[:, c[1]:c[2]], 8, A_DQK, LANE),
        w_in[:, c[2]:c[5]], _pad_chunks(w_in[:, c[5]:c[6]], 1, B_ROPE, LANE)], axis=1).astype(BF16)
    log2e = 1.0 / math.log(2.0)
    gq = jnp.tile(jnp.pad(a_qk_g[0] * (A_DQK ** -0.5 * log2e), (0, LANE - A_DQK)), 8)[None]
    gk = jnp.tile(jnp.pad(a_qk_g[1], (0, LANE - A_DQK)), 8)[None]
    wq = _pad_chunks(b_wq_up, B_HEADS, B_DQK, 256).astype(BF16)
    gbq = jnp.tile(jnp.pad(b_qk_g[0] * (B_DQK ** -0.5 * log2e), (0, 256 - B_DQK)), B_HEADS)[None]
    gbk = jnp.pad(b_qk_g[1], (0, 256 - B_DQK))[None]
    return dict(w_in=w_in_p, gq=gq, gk=gk, gqa=b_qa_g[None], wq=wq, gbq=gbq, gkva=b_kva_g[None]), gbk


def _tile(seq, cap):
    return min(seq, cap)


def _attn_layer(xp, xs, shapes, modp, mods, g1, wp, gbk, wkv, lam_p, gsub, w_out, ctx, lam_init):
    (bp, sp), (bs, ss) = shapes
    ctx_ka, ctx_va, ctx_ckv, ctx_kr = ctx
    past = ctx_ka.shape[1]
    tabs_s = _rope_tables(ss)
    tabs_p = _rope_tables(sp)
    tmp, tms = _tile(sp, 256), _tile(ss, 512)
    qa, ka, va, qb, ckv, kr = _attn_in(xp, modp, g1, wp, tabs_p, seq=sp, per_batch_mod=False, rope=False,
                                      kv_dtype=F32, tm=tmp)
    kb, vb = _mla_expand(ckv, kr, wkv, gbk, tabs_p, seq=sp, rope=False, tm=tmp)
    r3 = lambda a, b_, s_: a.reshape(b_, s_, a.shape[-1])
    o_p = _attention(r3(qa, bp, sp), r3(qb, bp, sp), lam_p, gsub,
                     [(r3(ka, bp, sp), r3(kb, bp, sp), _values_t(r3(va, bp, sp), r3(vb, bp, sp)))],
                     lam_init=lam_init, tq=_tile(sp, 256), nb=2 if bp % 2 == 0 else 1)
    new_ak = ka.reshape(bp, sp, 2 * A_HEADS, LANE)[..., :A_DQK].reshape(bp, 1, sp, A_HEADS, 2, A_DQK)
    new_av = va.reshape(bp, 1, sp, A_HEADS, A_DV)
    new_ckv = ckv.reshape(bp, 1, sp, B_KV_LORA)
    new_kr = kr[:, :B_ROPE].reshape(bp, 1, sp, B_ROPE)
    xp1 = _proj_res(o_p.reshape(bp * sp, D), xp, modp, w_out, seq=sp, per_batch_mod=False, gate_row=2, tm=tmp)
    qa, ka, va, qb, ckv, kr = _attn_in(xs, mods, g1, wp, tabs_s, seq=ss, per_batch_mod=True, rope=True,
                                      kv_dtype=BF16, tm=tms)
    kb, vb = _mla_expand(ckv, kr, wkv, gbk, tabs_s, seq=ss, rope=True, tm=tms)
    tpast = _tile(past, 512)
    tabs_c = _rope_tables(past)
    kb_c, vb_c = _mla_expand(ctx_ckv.reshape(bs * past, B_KV_LORA),
                             jnp.pad(ctx_kr.reshape(bs * past, B_ROPE), ((0, 0), (0, LANE - B_ROPE))),
                             wkv, gbk, tabs_c, seq=past, rope=False, tm=tpast)
    ka_c = _pad_chunks(ctx_ka.reshape(bs, past, 2 * A_HEADS * A_DQK), 2 * A_HEADS, A_DQK, LANE).astype(BF16)
    va_c = ctx_va.reshape(bs, past, A_HEADS * A_DV)
    o_s = _attention(r3(qa, bs, ss), r3(qb, bs, ss), lam_p, gsub,
                     [(ka_c, r3(kb_c, bs, past), _values_t(va_c, r3(vb_c, bs, past))),
                      (r3(ka, bs, ss), r3(kb, bs, ss), _values_t(r3(va, bs, ss), r3(vb, bs, ss)))],
                     lam_init=lam_init, tq=_tile(ss, 512))
    xs1 = _proj_res(o_s.reshape(bs * ss, D), xs, mods, w_out, seq=ss, per_batch_mod=True, gate_row=2, tm=tms)
    return xp1, xs1, (new_ak, new_av, new_ckv, new_kr)


def _ssd_stream(x, mod, g1, w_in, conv_w, conv_b, bias, alog, dvec, ng, w_out, s0f, s0b, *, bsz, seq, per_batch_mod, tm):
    z, xbc, dt = _ssd_in(x, mod, g1, w_in, seq=seq, per_batch_mod=per_batch_mod, tm=tm)
    xbc = _conv_silu(xbc.reshape(bsz, seq, SSD_CONV_DIM), conv_w, conv_b, cw=512)
    dt = dt.reshape(bsz, seq, 2 * LANE)
    yf, yb, sf, sb = _ssd_scan(xbc, dt, bias, alog, s0f, s0b)
    flat = lambda a: a.reshape(bsz * seq, a.shape[-1])
    x1 = _ssd_out(flat(yf), flat(yb), flat(xbc), dvec, z, x, mod, ng, w_out,
                  seq=seq, per_batch_mod=per_batch_mod, tm=tm)
    return x1, sf, sb


def kernel(x_prompt, x_sample, cache_a_k, cache_a_v, cache_b_ckv, cache_b_krope, state_ssd, c, c_ctx, mod_w, mod_b, norm1_g, norm2_g, attn_w_in, a_qk_g, a_lambda, a_sub_g, b_qa_g, b_wq_up, b_kva_g, b_wkv_up, b_qk_g, attn_w_out, ssd_w_in, ssd_conv_w, ssd_conv_b, ssd_dt_bias, ssd_a_log, ssd_d, ssd_norm_g, ssd_w_out, router_w, router_bias, moe_w_in, moe_w_out, shared_w_in, shared_w_out):
    bp, sp, _ = x_prompt.shape
    bs, ss, _ = x_sample.shape
    depth = mod_w.shape[0]
    shapes = ((bp, sp), (bs, ss))
    xp = x_prompt.reshape(bp * sp, D)
    xs = x_sample.reshape(bs * ss, D)

    n_c = 1 + bs
    rows = -(-n_c // 16) * 16
    c_all = jnp.pad(jnp.concatenate([c_ctx[None], c], axis=0), ((0, rows - n_c), (0, 0)))
    mod = _mod_vectors(c_all, mod_w, mod_b).reshape(depth, rows, 6, D)

    rw_t = jnp.pad(router_w.T, ((0, LANE - N_EXPERTS), (0, 0)))
    rb_col = jnp.pad(router_bias, (0, LANE - N_EXPERTS))[:, None]
    tmp, tms = _tile(sp, 256), _tile(ss, 512)

    new_caches = []
    new_ssd = []
    for l in range(depth):
        i = l // 2
        modp = mod[l, 0:1]
        mods = mod[l, 1:1 + bs]
        g1 = norm1_g[l][None]
        g2 = norm2_g[l][None]
        if l % 2 == 0:
            lam_init = 0.8 - 0.6 * math.exp(-0.3 * l)
            wp, gbk = _attn_weights(attn_w_in[i], a_qk_g[i], b_qa_g[i], b_wq_up[i], b_kva_g[i], b_qk_g[i])
            ctx = (cache_a_k[:, i], cache_a_v[:, i], cache_b_ckv[:, i], cache_b_krope[:, i])
            xp, xs, caches = _attn_layer(xp, xs, shapes, modp, mods, g1, wp, gbk, b_wkv_up[i].astype(BF16),
                                         a_lambda[i], a_sub_g[i][None], attn_w_out[i].astype(BF16), ctx, lam_init)
            new_caches.append(caches)
        else:
            w_in = ssd_w_in[i]
            n0 = SSD_DI + SSD_CONV_DIM
            w_in_p = jnp.concatenate([w_in[:, :n0], _pad_chunks(w_in[:, n0:], 2, SSD_H, LANE)], axis=1).astype(BF16)
            bias = _pad_chunks(ssd_dt_bias[i].reshape(1, 2 * SSD_H), 2, SSD_H, LANE)
            alog = _pad_chunks(ssd_a_log[i].reshape(1, 2 * SSD_H), 2, SSD_H, LANE)
            dvec = jnp.repeat(ssd_d[i], SSD_P)[None]
            args = (g1, w_in_p, ssd_conv_w[i], ssd_conv_b[i][None], bias, alog, dvec, ssd_norm_g[i][None],
                    ssd_w_out[i].astype(BF16))
            zero = jnp.zeros((bp, SSD_DI, SSD_N), F32)
            xp, sf, sb = _ssd_stream(xp, modp, *args, zero, zero, bsz=bp, seq=sp, per_batch_mod=False, tm=tmp)
            xs, _, _ = _ssd_stream(xs, mods, *args, state_ssd[:, i, 0].reshape(bs, SSD_DI, SSD_N),
                                   state_ssd[:, i, 1].reshape(bs, SSD_DI, SSD_N), bsz=bs, seq=ss,
                                   per_batch_mod=True, tm=tms)
            new_ssd.append(jnp.stack([sf, sb], axis=1).reshape(bp, 2, SSD_H, SSD_P, SSD_N))
        expert_w = (l, moe_w_in, moe_w_out, shared_w_in, shared_w_out)
        xp = _moe_block(xp, modp, g2, rw_t, rb_col, expert_w, seq=sp, per_batch_mod=False, tm=tmp)
        xs = _moe_block(xs, mods, g2, rw_t, rb_col, expert_w, seq=ss, per_batch_mod=True, tm=tms)

    cat = lambda k: jnp.concatenate([cc[k] for cc in new_caches], axis=1)
    return (xp.reshape(bp, sp, D), xs.reshape(bs, ss, D), cat(0), cat(1), cat(2), cat(3),
            jnp.stack(new_ssd, axis=1))
```
